```python
import jax, jax.numpy as jnp
from jax import lax
import numpy as np

D_MODEL = 2048
BATCH = 4
SEQ = 2048
DEPTH = 4
DEC_BATCH = 128
DEC_SEQ = 8
PAST_LEN = 16384
PAGE_SIZE = 128

HEAD_DIM = 128
H_A = 6
H_B = 5
H_C = 5
D_A = H_A * HEAD_DIM
D_B = H_B * HEAD_DIM
D_C = H_C * HEAD_DIM
CONV_A_WIDTH = 31
CONV_B_WIDTH = 3
CHUNK = 128
IN_COLS = 2 * D_A + 3 * D_B + 2 * D_C
SPLIT_IDX = [D_A, 2 * D_A, 2 * D_A + D_B, 2 * D_A + 2 * D_B, 2 * D_A + 3 * D_B,
             2 * D_A + 3 * D_B + D_C]
D_FF = 5632
N_EXPERTS = 8
TOP_K = 2
D_FF_EXPERT = 7168
N_DENSE = (DEPTH + 1) // 2
N_MOE = DEPTH // 2
RMS_EPS = 1e-6
LN_EPS = 1e-5

kernel_name = "hybrid_conformer_shortconv_chunkmlp_decoder_step"


def rmsnorm(x, g):
    xf = x.astype(jnp.float32)
    y = xf * lax.rsqrt(jnp.mean(xf * xf, axis=-1, keepdims=True) + RMS_EPS)
    return (y * g.astype(jnp.float32)).astype(x.dtype)


def layernorm(x, g, b):
    xf = x.astype(jnp.float32)
    mu = jnp.mean(xf, axis=-1, keepdims=True)
    var = jnp.mean(jnp.square(xf - mu), axis=-1, keepdims=True)
    y = (xf - mu) * lax.rsqrt(var + LN_EPS)
    return (y * g.astype(jnp.float32) + b.astype(jnp.float32)).astype(x.dtype)


def causal_dwconv(hist, x, w):
    K, C = w.shape
    xp = jnp.concatenate([hist.astype(x.dtype), x], axis=1)
    y = lax.conv_general_dilated(xp, w[:, None, :].astype(x.dtype), window_strides=(1,),
                                 padding='VALID', dimension_numbers=('NWC', 'WIO', 'NWC'),
                                 feature_group_count=C)
    return y, xp[:, xp.shape[1] - (K - 1):]


def chunk_spatial_gate(u, v, w_s, b_s):
    N, L, _ = v.shape
    n_chunks = -(-L // CHUNK)
    pad = n_chunks * CHUNK - L
    vp = jnp.pad(v, ((0, 0), (0, pad), (0, 0))).reshape(N, n_chunks, CHUNK, H_C, HEAD_DIM)
    mask = jnp.tril(jnp.ones((CHUNK, CHUNK), dtype=bool))
    w = jnp.where(mask[None], w_s, 0).astype(v.dtype)
    mixed = jnp.einsum('hts,bnshd->bnthd', w, vp) + b_s.T.astype(v.dtype)[None, None, :, :, None]
    mixed = mixed.reshape(N, n_chunks * CHUNK, D_C)[:, :L]
    return u * mixed


def token_mixers(h, hist_a, hist_b, w_in, w_out, conv_a_w, conv_a_b, ln_a_g, ln_a_b,
                 conv_b_w, ln_c_g, ln_c_b, spatial_w, spatial_b, group_g):
    z = h @ w_in
    a_val, a_gate, b_gate, c_gate, b_in, c_u, c_v = jnp.split(z, SPLIT_IDX, axis=-1)
    glu = a_val * jax.nn.sigmoid(a_gate)
    ya, new_a = causal_dwconv(hist_a, glu, conv_a_w)
    ya = jax.nn.silu(layernorm(ya + conv_a_b.astype(ya.dtype), ln_a_g, ln_a_b))
    pre = c_gate * b_in
    yb, new_b = causal_dwconv(hist_b, pre, conv_b_w)
    yb = b_gate * yb
    u = jax.nn.gelu(c_u)
    v = layernorm(jax.nn.gelu(c_v), ln_c_g, ln_c_b)
    yc = chunk_spatial_gate(u, v, spatial_w, spatial_b)
    L = v.shape[1]
    chunk_start = ((L - 1) // CHUNK) * CHUNK
    new_v = v[:, chunk_start:]
    y = jnp.concatenate([rmsnorm(ya, group_g[:D_A]),
                         rmsnorm(yb, group_g[D_A:D_A + D_B]),
                         rmsnorm(yc, group_g[D_A + D_B:])], axis=-1)
    return y @ w_out, new_a, new_b, new_v


def swiglu(h, w1, w3, w2):
    return (jax.nn.silu(h @ w1) * (h @ w3)) @ w2


def moe_swiglu(h, router_w, w1, w3, w2):
    shp = h.shape
    t = h.reshape(-1, shp[-1])
    logits = (t @ router_w).astype(jnp.float32)
    top_v, top_i = lax.top_k(logits, TOP_K)
    gates = jax.nn.softmax(top_v, axis=-1)
    comb = jnp.sum(jax.nn.one_hot(top_i, N_EXPERTS, dtype=jnp.float32) * gates[..., None], axis=1)
    comb = comb.astype(t.dtype)
    out = jnp.zeros_like(t)
    for e in range(N_EXPERTS):
        out = out + comb[:, e:e + 1] * swiglu(t, w1[e], w3[e], w2[e])
    return out.reshape(shp)


def run_trunk(x, hist_a_all, hist_b_all, norm1_g, norm2_g, final_g, w_in, w_out,
              conv_a_w, conv_a_b, ln_a_g, ln_a_b, conv_b_w, ln_c_g, ln_c_b,
              spatial_w, spatial_b, group_g, ffn_w1, ffn_w3, ffn_w2,
              router_w, moe_w1, moe_w3, moe_w2):
    sa, sb, sv = [], [], []
    for l in range(DEPTH):
        h = rmsnorm(x, norm1_g[l])
        y, na, nb, nv = token_mixers(h, hist_a_all[l], hist_b_all[l], w_in[l], w_out[l],
                                     conv_a_w[l], conv_a_b[l], ln_a_g[l], ln_a_b[l],
                                     conv_b_w[l], ln_c_g[l], ln_c_b[l],
                                     spatial_w[l], spatial_b[l], group_g[l])
        x = x + y
        h = rmsnorm(x, norm2_g[l])
        if l % 2 == 0:
            i = l // 2
            x = x + swiglu(h, ffn_w1[i], ffn_w3[i], ffn_w2[i])
        else:
            i = l // 2
            x = x + moe_swiglu(h, router_w[i], moe_w1[i], moe_w3[i], moe_w2[i])
        sa.append(na); sb.append(nb); sv.append(nv)
    return rmsnorm(x, final_g), jnp.stack(sa), jnp.stack(sb), jnp.stack(sv)


def setup_inputs(seed: int = 0) -> dict:
    key = jax.random.key(seed)
    ks = jax.random.split(key, 32)

    def nrm(k, shape, scale):
        return jax.random.normal(k, shape, jnp.float32) * scale

    return {
        "x_prompt": nrm(ks[0], (BATCH, SEQ, D_MODEL), 1.0),
        "x_sample": nrm(ks[1], (DEC_BATCH, DEC_SEQ, D_MODEL), 1.0),
        "state_conv_a": nrm(ks[2], (DEPTH, DEC_BATCH, CONV_A_WIDTH - 1, D_A), 0.5),
        "state_conv_b": nrm(ks[3], (DEPTH, DEC_BATCH, CONV_B_WIDTH - 1, D_B), 0.5),
        "norm1_g": 1.0 + nrm(ks[4], (DEPTH, D_MODEL), 0.02),
        "norm2_g": 1.0 + nrm(ks[5], (DEPTH, D_MODEL), 0.02),
        "final_g": 1.0 + nrm(ks[6], (D_MODEL,), 0.02),
        "w_in": nrm(ks[7], (DEPTH, D_MODEL, IN_COLS), D_MODEL ** -0.5),
        "w_out": nrm(ks[8], (DEPTH, D_MODEL, D_MODEL), D_MODEL ** -0.5),
        "conv_a_w": nrm(ks[9], (DEPTH, CONV_A_WIDTH, D_A), CONV_A_WIDTH ** -0.5),
        "conv_a_b": nrm(ks[10], (DEPTH, D_A), 0.02),
        "ln_a_g": 1.0 + nrm(ks[11], (DEPTH, D_A), 0.02),
        "ln_a_b": nrm(ks[12], (DEPTH, D_A), 0.02),
        "conv_b_w": nrm(ks[13], (DEPTH, CONV_B_WIDTH, D_B), CONV_B_WIDTH ** -0.5),
        "ln_c_g": 1.0 + nrm(ks[14], (DEPTH, D_C), 0.02),
        "ln_c_b": nrm(ks[15], (DEPTH, D_C), 0.02),
        "spatial_w": nrm(ks[16], (DEPTH, H_C, CHUNK, CHUNK), CHUNK ** -0.5),
        "spatial_b": 1.0 + nrm(ks[17], (DEPTH, H_C, CHUNK), 0.1),
        "group_g": 1.0 + nrm(ks[18], (DEPTH, D_MODEL), 0.02),
        "ffn_w1": nrm(ks[19], (N_DENSE, D_MODEL, D_FF), D_MODEL ** -0.5),
        "ffn_w3": nrm(ks[20], (N_DENSE, D_MODEL, D_FF), D_MODEL ** -0.5),
        "ffn_w2": nrm(ks[21], (N_DENSE, D_FF, D_MODEL), D_FF ** -0.5),
        "router_w": nrm(ks[22], (N_MOE, D_MODEL, N_EXPERTS), D_MODEL ** -0.5),
        "moe_w1": nrm(ks[23], (N_MOE, N_EXPERTS, D_MODEL, D_FF_EXPERT), D_MODEL ** -0.5),
        "moe_w3": nrm(ks[24], (N_MOE, N_EXPERTS, D_MODEL, D_FF_EXPERT), D_MODEL ** -0.5),
        "moe_w2": nrm(ks[25], (N_MOE, N_EXPERTS, D_FF_EXPERT, D_MODEL), D_FF_EXPERT ** -0.5),
    }


def reference(x_prompt, x_sample, state_conv_a, state_conv_b, norm1_g, norm2_g, final_g,
              w_in, w_out, conv_a_w, conv_a_b, ln_a_g, ln_a_b, conv_b_w, ln_c_g, ln_c_b,
              spatial_w, spatial_b, group_g, ffn_w1, ffn_w3, ffn_w2,
              router_w, moe_w1, moe_w3, moe_w2):
    weights = (norm1_g, norm2_g, final_g, w_in, w_out, conv_a_w, conv_a_b, ln_a_g, ln_a_b,
               conv_b_w, ln_c_g, ln_c_b, spatial_w, spatial_b, group_g,
               ffn_w1, ffn_w3, ffn_w2, router_w, moe_w1, moe_w3, moe_w2)
    zero_a = jnp.zeros((DEPTH, x_prompt.shape[0], CONV_A_WIDTH - 1, D_A), x_prompt.dtype)
    zero_b = jnp.zeros((DEPTH, x_prompt.shape[0], CONV_B_WIDTH - 1, D_B), x_prompt.dtype)
    y_prompt, pa, pb, pv = run_trunk(x_prompt, zero_a, zero_b, *weights)
    y_sample, sa, sb, sv = run_trunk(x_sample, state_conv_a, state_conv_b, *weights)
    return (y_prompt, y_sample, pa, pb, pv, sa, sb, sv)
```

```python
import functools

import jax
import jax.numpy as jnp
from jax import lax
from jax.experimental import pallas as pl
from jax.experimental.pallas import tpu as pltpu

F32 = jnp.float32
BF16 = jnp.bfloat16

D_MODEL = 2048
DEPTH = 4
HEAD_DIM = 128
D_A = 6 * HEAD_DIM
D_B = 5 * HEAD_DIM
D_C = 5 * HEAD_DIM
H_C = 5
CONV_A_WIDTH = 31
CONV_B_WIDTH = 3
CHUNK = 128
IN_COLS = 2 * D_A + 3 * D_B + 2 * D_C
N_EXPERTS = 8
TOP_K = 2
RMS_EPS = 1e-6
LN_EPS = 1e-5

O_AVAL = 0
O_AGATE = D_A
O_BGATE = 2 * D_A
O_CGATE = 2 * D_A + D_B
O_BIN = 2 * D_A + 2 * D_B
O_CU = 2 * D_A + 3 * D_B
O_CV = 2 * D_A + 3 * D_B + D_C

LANES = 128
SUBLANES = 8
VMEM_LIMIT_BYTES = 56 * 1024 * 1024

TM_UP = 1024
TN_UP = 512
TM_DOWN = 512
TK_DOWN = 512
TM_MOE = 1024
TM_ROUTER = 512
TM_ROWS = 256
HALO = 32


def _params(sem):
    return pltpu.CompilerParams(dimension_semantics=sem, vmem_limit_bytes=VMEM_LIMIT_BYTES)


def _rms(x):
    return x * lax.rsqrt(jnp.mean(x * x, axis=-1, keepdims=True) + RMS_EPS)


def _ln(x, g, b):
    mu = jnp.mean(x, axis=-1, keepdims=True)
    xc = x - mu
    var = jnp.mean(xc * xc, axis=-1, keepdims=True)
    return xc * lax.rsqrt(var + LN_EPS) * g + b


def _norm_body(x_ref, g_ref, o_ref):
    o_ref[...] = (_rms(x_ref[...]) * g_ref[...]).astype(o_ref.dtype)


def _norm(x, g):
    t, d = x.shape
    return pl.pallas_call(
        _norm_body,
        grid=(t // TM_DOWN,),
        in_specs=[pl.BlockSpec((TM_DOWN, d), lambda i: (i, 0)),
                  pl.BlockSpec((1, d), lambda i: (0, 0))],
        out_specs=pl.BlockSpec((TM_DOWN, d), lambda i: (i, 0)),
        out_shape=jax.ShapeDtypeStruct((t, d), BF16),
        compiler_params=_params(("arbitrary",)),
        name="norm_in",
    )(x, g.reshape(1, d))


def _up_compute(x_ref, wbf, o_ref, n_w):
    x = x_ref[...].astype(BF16)
    a = jnp.dot(x, wbf[0], preferred_element_type=F32)
    if n_w == 2:
        b = jnp.dot(x, wbf[1], preferred_element_type=F32)
        a = jax.nn.silu(a) * b
    o_ref[...] = a.astype(o_ref.dtype)


def _mm_up_body(x_ref, *refs, n_w):
    w_refs, o_ref, wbf = refs[:n_w], refs[n_w], refs[n_w + 1]

    @pl.when(pl.program_id(1) == 0)
    def _():
        for n in range(n_w):
            wbf[n] = w_refs[n][...].astype(BF16)

    _up_compute(x_ref, wbf, o_ref, n_w)


def _mm_up(x, ws, out_dtype):
    t, k = x.shape
    n = ws[0].shape[1]
    n_w = len(ws)
    grid = (pl.cdiv(n, TN_UP), t // TM_UP)
    return pl.pallas_call(
        functools.partial(_mm_up_body, n_w=n_w),
        grid=grid,
        in_specs=[pl.BlockSpec((TM_UP, k), lambda j, i: (i, 0))]
        + [pl.BlockSpec((k, TN_UP), lambda j, i: (0, j)) for _ in ws],
        out_specs=pl.BlockSpec((TM_UP, TN_UP), lambda j, i: (i, j)),
        out_shape=jax.ShapeDtypeStruct((t, n), out_dtype),
        scratch_shapes=[pltpu.VMEM((n_w, k, TN_UP), BF16)],
        compiler_params=_params(("arbitrary", "arbitrary")),
        name="mm_up%d" % n_w,
    )(x, *ws)


def _mm_down_body(a_ref, w_ref, x_ref, g_ref, xo_ref, ho_ref, acc, *, nk):
    k = pl.program_id(1)
    p = jnp.dot(a_ref[...].astype(BF16), w_ref[...].astype(BF16), preferred_element_type=F32)

    @pl.when(k == 0)
    def _():
        acc[...] = p

    @pl.when(k > 0)
    def _():
        acc[...] += p

    @pl.when(k == nk - 1)
    def _():
        xn = x_ref[...] + acc[...]
        xo_ref[...] = xn
        ho_ref[...] = (_rms(xn) * g_ref[...]).astype(ho_ref.dtype)


def _mm_down(a, w, x, g, h_dtype):
    t, kdim = a.shape
    d = w.shape[1]
    nk = kdim // TK_DOWN
    return pl.pallas_call(
        functools.partial(_mm_down_body, nk=nk),
        grid=(t // TM_DOWN, nk),
        in_specs=[pl.BlockSpec((TM_DOWN, TK_DOWN), lambda i, k: (i, k)),
                  pl.BlockSpec((TK_DOWN, d), lambda i, k: (k, 0)),
                  pl.BlockSpec((TM_DOWN, d), lambda i, k: (i, 0)),
                  pl.BlockSpec((1, d), lambda i, k: (0, 0))],
        out_specs=[pl.BlockSpec((TM_DOWN, d), lambda i, k: (i, 0)),
                   pl.BlockSpec((TM_DOWN, d), lambda i, k: (i, 0))],
        out_shape=[jax.ShapeDtypeStruct((t, d), F32), jax.ShapeDtypeStruct((t, d), h_dtype)],
        scratch_shapes=[pltpu.VMEM((TM_DOWN, d), F32)],
        compiler_params=_params(("arbitrary", "arbitrary")),
        name="mm_down",
    )(a, w, x, g.reshape(1, d))


def _mixer_tail(ya, yb, c_u, c_v, cab_ref, lag_ref, lab_ref, lcg_ref, lcb_ref, gg_ref,
                mixw_ref, mixb_ref, y_ref, v_ref):
    rows = ya.shape[0]
    ya = jax.nn.silu(_ln(ya + cab_ref[...], lag_ref[...], lab_ref[...]))
    y_ref[:, 0:D_A] = (_rms(ya) * gg_ref[:, 0:D_A]).astype(y_ref.dtype)
    y_ref[:, D_A:D_A + D_B] = (_rms(yb) * gg_ref[:, D_A:D_A + D_B]).astype(y_ref.dtype)

    u = jax.nn.gelu(c_u)
    v = _ln(jax.nn.gelu(c_v), lcg_ref[...], lcb_ref[...])
    v_ref[...] = v
    vb = v.astype(BF16)
    for c in range(rows // CHUNK):
        r0 = c * CHUNK
        heads = []
        for h in range(H_C):
            l0 = h * HEAD_DIM
            mixed = jnp.dot(mixw_ref[h], vb[r0:r0 + CHUNK, l0:l0 + HEAD_DIM],
                            preferred_element_type=F32) + mixb_ref[h]
            heads.append(u[r0:r0 + CHUNK, l0:l0 + HEAD_DIM] * mixed)
        yc = jnp.concatenate(heads, axis=1)
        y_ref[r0:r0 + CHUNK, D_A + D_B:D_MODEL] = (
            _rms(yc) * gg_ref[:, D_A + D_B:D_MODEL]).astype(y_ref.dtype)


def _mixer_prompt_body(z_ref, zh_ref, caw_ref, cab_ref, lag_ref, lab_ref, cbw_ref, lcg_ref,
                       lcb_ref, gg_ref, mixw_ref, mixb_ref,
                       y_ref, glu_ref, pre_ref, v_ref, exta, extb):
    tq = z_ref.shape[0]
    has_prev = pl.program_id(1) > 0

    glu = z_ref[:, O_AVAL:O_AVAL + D_A] * jax.nn.sigmoid(z_ref[:, O_AGATE:O_AGATE + D_A])
    glu_ref[...] = glu
    exta[0:HALO, :] = jnp.where(has_prev, zh_ref[:, O_AVAL:O_AVAL + D_A]
                                * jax.nn.sigmoid(zh_ref[:, O_AGATE:O_AGATE + D_A]), 0.0)
    exta[HALO:HALO + tq, :] = glu
    off = HALO - (CONV_A_WIDTH - 1)
    ya = caw_ref[0:1, :] * exta[off:off + tq, :]
    for k in range(1, CONV_A_WIDTH):
        ya = ya + caw_ref[k:k + 1, :] * exta[off + k:off + k + tq, :]

    pre = z_ref[:, O_CGATE:O_CGATE + D_B] * z_ref[:, O_BIN:O_BIN + D_B]
    pre_ref[...] = pre
    extb[0:SUBLANES, :] = jnp.where(has_prev, zh_ref[HALO - SUBLANES:HALO, O_CGATE:O_CGATE + D_B]
                                    * zh_ref[HALO - SUBLANES:HALO, O_BIN:O_BIN + D_B], 0.0)
    extb[SUBLANES:SUBLANES + tq, :] = pre
    off = SUBLANES - (CONV_B_WIDTH - 1)
    yb = cbw_ref[0:1, :] * extb[off:off + tq, :]
    for k in range(1, CONV_B_WIDTH):
        yb = yb + cbw_ref[k:k + 1, :] * extb[off + k:off + k + tq, :]
    yb = z_ref[:, O_BGATE:O_BGATE + D_B] * yb

    _mixer_tail(ya, yb, z_ref[:, O_CU:O_CU + D_C], z_ref[:, O_CV:O_CV + D_C],
                cab_ref, lag_ref, lab_ref, lcg_ref, lcb_ref, gg_ref, mixw_ref, mixb_ref,
                y_ref, v_ref)


def _shift_rows_in_seq(x, d, row_in_seq):
    if d == 0:
        return x
    return jnp.where(row_in_seq >= d, pltpu.roll(x, d, axis=0), 0.0)


def _mixer_sample_body(z_ref, ha_ref, hb_ref, caw_ref, cab_ref, lag_ref, lab_ref, cbw_ref,
                       lcg_ref, lcb_ref, gg_ref, mixw_ref, mixb_ref,
                       y_in, glu_in, pre_in, v_in,
                       y_ref, glu_ref, pre_ref, v_ref, hista):
    del y_in, glu_in, pre_in, v_in
    rows = z_ref.shape[0]
    nseq, seq = rows // SUBLANES, SUBLANES
    hist_len = CONV_A_WIDTH - 1

    glu = z_ref[:, O_AVAL:O_AVAL + D_A] * jax.nn.sigmoid(z_ref[:, O_AGATE:O_AGATE + D_A])
    glu_ref[...] = glu
    hista[...] = jnp.zeros_like(hista)
    hista[:, 0:hist_len, :] = ha_ref[...]
    row_a = lax.broadcasted_iota(jnp.int32, (rows, D_A), 0) % seq
    ya = caw_ref[hist_len:hist_len + 1, :] * glu
    for d in range(1, seq):
        ya = ya + caw_ref[hist_len - d:hist_len - d + 1, :] * _shift_rows_in_seq(glu, d, row_a)
    for k in range(hist_len):
        ya = ya + caw_ref[k:k + 1, :] * hista[:, k:k + seq, :].reshape(rows, D_A)

    pre = z_ref[:, O_CGATE:O_CGATE + D_B] * z_ref[:, O_BIN:O_BIN + D_B]
    pre_ref[...] = pre
    row_b = lax.broadcasted_iota(jnp.int32, (rows, D_B), 0) % seq
    h0 = jnp.broadcast_to(hb_ref[:, 0:1, :], (nseq, seq, D_B)).reshape(rows, D_B)
    h1 = jnp.broadcast_to(hb_ref[:, 1:2, :], (nseq, seq, D_B)).reshape(rows, D_B)
    x1 = jnp.where(row_b == 0, h1, _shift_rows_in_seq(pre, 1, row_b))
    x0 = jnp.where(row_b == 0, h0, jnp.where(row_b == 1, h1, _shift_rows_in_seq(pre, 2, row_b)))
    yb = cbw_ref[0:1, :] * x0 + cbw_ref[1:2, :] * x1 + cbw_ref[2:3, :] * pre
    yb = z_ref[:, O_BGATE:O_BGATE + D_B] * yb

    _mixer_tail(ya, yb, z_ref[:, O_CU:O_CU + D_C], z_ref[:, O_CV:O_CV + D_C],
                cab_ref, lag_ref, lab_ref, lcg_ref, lcb_ref, gg_ref, mixw_ref, mixb_ref,
                y_ref, v_ref)


def _full(shape):
    nd = len(shape)
    return pl.BlockSpec(shape, lambda *_: (0,) * nd)


def _mixers(z, hist_a, hist_b, lw, n_prompt, seq_len):
    t = z.shape[0]
    tq = TM_ROWS
    blocks_per_seq = seq_len // tq
    n_seq = n_prompt // seq_len
    out_shape = [jax.ShapeDtypeStruct((t, D_MODEL), BF16), jax.ShapeDtypeStruct((t, D_A), F32),
                 jax.ShapeDtypeStruct((t, D_B), F32), jax.ShapeDtypeStruct((t, D_C), F32)]
    widths = (D_MODEL, D_A, D_B, D_C)
    param_specs_a = [_full((CONV_A_WIDTH, D_A)), _full((1, D_A)), _full((1, D_A)), _full((1, D_A))]
    param_specs_b = [_full((CONV_B_WIDTH, D_B)), _full((1, D_C)), _full((1, D_C)),
                     _full((1, D_MODEL)), _full((H_C, CHUNK, CHUNK)), _full((H_C, CHUNK, CHUNK))]
    common = (lw["conv_a_w"], lw["conv_a_b"], lw["ln_a_g"], lw["ln_a_b"], lw["conv_b_w"],
              lw["ln_c_g"], lw["ln_c_b"], lw["group_g"])

    def row_block(b, j):
        return b * blocks_per_seq + j

    halo_per_block = tq // HALO
    outs = pl.pallas_call(
        _mixer_prompt_body,
        grid=(n_seq, blocks_per_seq),
        in_specs=[pl.BlockSpec((tq, IN_COLS), lambda b, j: (row_block(b, j), 0)),
                  pl.BlockSpec((HALO, IN_COLS),
                               lambda b, j: (jnp.maximum(row_block(b, j) * halo_per_block - 1, 0), 0))]
        + param_specs_a + param_specs_b,
        out_specs=[pl.BlockSpec((tq, w), lambda b, j: (row_block(b, j), 0)) for w in widths],
        out_shape=out_shape,
        scratch_shapes=[pltpu.VMEM((HALO + tq, D_A), F32), pltpu.VMEM((SUBLANES + tq, D_B), F32)],
        compiler_params=_params(("arbitrary", "arbitrary")),
        name="mixer_prompt",
    )(z, z, *common, lw["mix_w_prompt"], lw["mix_b_prompt"])

    first = n_prompt // tq
    n_in = 3 + len(common) + 2
    nseq_blk = tq // SUBLANES
    outs = pl.pallas_call(
        _mixer_sample_body,
        grid=((t - n_prompt) // tq,),
        in_specs=[pl.BlockSpec((tq, IN_COLS), lambda i: (first + i, 0)),
                  pl.BlockSpec((nseq_blk, CONV_A_WIDTH - 1, D_A), lambda i: (i, 0, 0)),
                  pl.BlockSpec((nseq_blk, CONV_B_WIDTH - 1, D_B), lambda i: (i, 0, 0))]
        + param_specs_a + param_specs_b
        + [pl.BlockSpec(memory_space=pl.ANY)] * 4,
        out_specs=[pl.BlockSpec((tq, w), lambda i: (first + i, 0)) for w in widths],
        out_shape=out_shape,
        input_output_aliases={n_in + n: n for n in range(4)},
        scratch_shapes=[pltpu.VMEM((nseq_blk, CONV_A_WIDTH - 1 + SUBLANES + 2, D_A), F32)],
        compiler_params=_params(("arbitrary",)),
        name="mixer_sample",
    )(z, hist_a, hist_b, *common, lw["mix_w_sample"], lw["mix_b_sample"], *outs)
    return outs


def _router_body(h_ref, rw_ref, meta_ref, cnt_ref, carry):
    @pl.when(pl.program_id(0) == 0)
    def _():
        carry[...] = jnp.zeros_like(carry)

    tm = h_ref.shape[0]
    logits = jnp.dot(h_ref[...], rw_ref[...], precision=lax.Precision.HIGHEST,
                     preferred_element_type=F32)
    lane = lax.broadcasted_iota(jnp.int32, (tm, LANES), 1).astype(F32)
    neg = -jnp.inf
    l1 = jnp.where(lane < N_EXPERTS, logits, neg)
    m1 = jnp.max(l1, axis=1, keepdims=True)
    i1 = jnp.min(jnp.where(l1 == m1, lane, float(LANES)), axis=1, keepdims=True)
    l2 = jnp.where(lane == i1, neg, l1)
    m2 = jnp.max(l2, axis=1, keepdims=True)
    i2 = jnp.min(jnp.where(l2 == m2, lane, float(LANES)), axis=1, keepdims=True)
    e = jnp.exp(m2 - m1)
    den = 1.0 + e
    g1 = 1.0 / den
    g2 = e / den

    oh1 = lane == i1
    oh2 = lane == i2
    onehot = jnp.where(oh1, 1.0, jnp.where(oh2, 1.0, 0.0))
    r = lax.broadcasted_iota(jnp.int32, (tm, tm), 0)
    c = lax.broadcasted_iota(jnp.int32, (tm, tm), 1)
    earlier = jnp.where(c < r, 1.0, 0.0).astype(BF16)
    before = jnp.dot(earlier, onehot.astype(BF16), preferred_element_type=F32) + carry[...]
    r1 = jnp.sum(jnp.where(oh1, before, 0.0), axis=1, keepdims=True)
    r2 = jnp.sum(jnp.where(oh2, before, 0.0), axis=1, keepdims=True)
    carry[...] += jnp.sum(onehot, axis=0, keepdims=True)
    cnt_ref[...] = carry[...]

    meta = jnp.where(lane == 0, i1, jnp.where(lane == 1, i2, jnp.where(lane == 2, r1, jnp.where(
        lane == 3, r2, jnp.where(lane == 4, g1, jnp.where(lane == 5, g2, 0.0))))))
    meta_ref[...] = meta


def _router(h, router_w):
    t, d = h.shape
    rw = jnp.pad(router_w, ((0, 0), (0, LANES - N_EXPERTS)))
    return pl.pallas_call(
        _router_body,
        grid=(t // TM_ROUTER,),
        in_specs=[pl.BlockSpec((TM_ROUTER, d), lambda i: (i, 0)), _full((d, LANES))],
        out_specs=[pl.BlockSpec((TM_ROUTER, LANES), lambda i: (i, 0)), _full((1, LANES))],
        out_shape=[jax.ShapeDtypeStruct((t, LANES), F32), jax.ShapeDtypeStruct((1, LANES), F32)],
        scratch_shapes=[pltpu.VMEM((1, LANES), F32)],
        compiler_params=_params(("arbitrary",)),
        name="router",
    )(h, rw)


def _dispatch_body(pos, h_ref, xs_in, xs_out, sem):
    del xs_in
    base = pl.program_id(0) * TM_ROWS

    def copy(r, k):
        t = base + r
        return pltpu.make_async_copy(h_ref.at[pl.ds(t, 1), :],
                                     xs_out.at[pl.ds(pos[TOP_K * t + k], 1), :], sem)

    def issue(r, carry):
        for k in range(TOP_K):
            copy(r, k).start()
        return carry

    def wait(r, carry):
        for k in range(TOP_K):
            copy(r, k).wait()
        return carry

    lax.fori_loop(0, TM_ROWS, issue, 0)
    lax.fori_loop(0, TM_ROWS, wait, 0)


def _dispatch(pos_flat, h, n_rows):
    t, d = h.shape
    xs0 = jnp.zeros((n_rows, d), F32)
    return pl.pallas_call(
        _dispatch_body,
        grid_spec=pltpu.PrefetchScalarGridSpec(
            num_scalar_prefetch=1,
            grid=(t // TM_ROWS,),
            in_specs=[pl.BlockSpec(memory_space=pl.ANY), pl.BlockSpec(memory_space=pl.ANY)],
            out_specs=pl.BlockSpec(memory_space=pl.ANY),
            scratch_shapes=[pltpu.SemaphoreType.DMA(())]),
        out_shape=jax.ShapeDtypeStruct((n_rows, d), F32),
        input_output_aliases={2: 0},
        compiler_params=_params(("arbitrary",)),
        name="dispatch",
    )(pos_flat, h, xs0)


def _gmm_up_body(it_e, it_j, it_blk, it_first, it_valid, x_ref, w1_ref, w3_ref, o_ref, wbf):
    del it_e, it_j, it_blk
    n = pl.program_id(0)

    @pl.when(it_valid[n] == 1)
    def _():
        @pl.when(it_first[n] == 1)
        def _():
            wbf[0] = w1_ref[...].astype(BF16)
            wbf[1] = w3_ref[...].astype(BF16)

        _up_compute(x_ref, wbf, o_ref, 2)


def _gmm_up(items, xs, w1, w3):
    n_rows, k = xs.shape
    n_ff = w1.shape[2]
    n_items = items[0].shape[0]
    return pl.pallas_call(
        _gmm_up_body,
        grid_spec=pltpu.PrefetchScalarGridSpec(
            num_scalar_prefetch=5,
            grid=(n_items,),
            in_specs=[pl.BlockSpec((TM_MOE, k), lambda n, e, j, b, f, v: (b[n], 0)),
                      pl.BlockSpec((None, k, TN_UP), lambda n, e, j, b, f, v: (e[n], 0, j[n])),
                      pl.BlockSpec((None, k, TN_UP), lambda n, e, j, b, f, v: (e[n], 0, j[n]))],
            out_specs=pl.BlockSpec((TM_MOE, TN_UP), lambda n, e, j, b, f, v: (b[n], j[n])),
            scratch_shapes=[pltpu.VMEM((2, k, TN_UP), BF16)]),
        out_shape=jax.ShapeDtypeStruct((n_rows, n_ff), BF16),
        compiler_params=_params(("arbitrary",)),
        name="gmm_up",
    )(*items, xs, w1, w3)


def _gmm_down_body(t_e, t_blk, t_valid, a_ref, w_ref, o_ref, acc, *, nk):
    del t_e, t_blk
    n = pl.program_id(0)
    k = pl.program_id(1)

    @pl.when(t_valid[n] == 1)
    def _():
        p = jnp.dot(a_ref[...], w_ref[...].astype(BF16), preferred_element_type=F32)

        @pl.when(k == 0)
        def _():
            acc[...] = p

        @pl.when(k > 0)
        def _():
            acc[...] += p

        @pl.when(k == nk - 1)
        def _():
            o_ref[...] = acc[...]


def _gmm_down(tiles, a, w2):
    n_rows, n_ff = a.shape
    d = w2.shape[2]
    nk = n_ff // TK_DOWN
    n_tiles = tiles[0].shape[0]

    def kk(n, k, v):
        return jnp.where(v[n] == 1, k, nk - 1)

    return pl.pallas_call(
        functools.partial(_gmm_down_body, nk=nk),
        grid_spec=pltpu.PrefetchScalarGridSpec(
            num_scalar_prefetch=3,
            grid=(n_tiles, nk),
            in_specs=[pl.BlockSpec((TM_MOE, TK_DOWN), lambda n, k, e, b, v: (b[n], kk(n, k, v))),
                      pl.BlockSpec((None, TK_DOWN, d), lambda n, k, e, b, v: (e[n], kk(n, k, v), 0))],
            out_specs=pl.BlockSpec((TM_MOE, d), lambda n, k, e, b, v: (b[n], 0)),
            scratch_shapes=[pltpu.VMEM((TM_MOE, d), F32)]),
        out_shape=jax.ShapeDtypeStruct((n_rows, d), F32),
        compiler_params=_params(("arbitrary", "arbitrary")),
        name="gmm_down",
    )(*tiles, a, w2)


def _combine_body(pos, x_ref, meta_ref, g_ref, ys_ref, *refs, emit_x):
    out_refs, ybuf, sem = refs[:-2], refs[-2], refs[-1]
    base = pl.program_id(0) * TM_ROWS

    def copy(r, k):
        return pltpu.make_async_copy(ys_ref.at[pl.ds(pos[TOP_K * (base + r) + k], 1), :],
                                     ybuf.at[k, pl.ds(r, 1), :], sem)

    def issue(r, carry):
        for k in range(TOP_K):
            copy(r, k).start()
        return carry

    def wait(r, carry):
        for k in range(TOP_K):
            copy(r, k).wait()
        return carry

    lax.fori_loop(0, TM_ROWS, issue, 0)
    lax.fori_loop(0, TM_ROWS, wait, 0)

    xn = x_ref[...] + (meta_ref[:, 4:5] * ybuf[0] + meta_ref[:, 5:6] * ybuf[1])
    h = _rms(xn) * g_ref[...]
    if emit_x:
        out_refs[0][...] = xn
        out_refs[1][...] = h.astype(out_refs[1].dtype)
    else:
        out_refs[0][...] = h.astype(out_refs[0].dtype)


def _combine(pos_flat, x, meta, g, ys, emit_x):
    t, d = x.shape
    blk = pl.BlockSpec((TM_ROWS, d), lambda i, p: (i, 0))
    if emit_x:
        out_specs = [blk, blk]
        out_shape = [jax.ShapeDtypeStruct((t, d), F32), jax.ShapeDtypeStruct((t, d), BF16)]
    else:
        out_specs = [blk]
        out_shape = [jax.ShapeDtypeStruct((t, d), F32)]
    return pl.pallas_call(
        functools.partial(_combine_body, emit_x=emit_x),
        grid_spec=pltpu.PrefetchScalarGridSpec(
            num_scalar_prefetch=1,
            grid=(t // TM_ROWS,),
            in_specs=[blk, pl.BlockSpec((TM_ROWS, LANES), lambda i, p: (i, 0)),
                      pl.BlockSpec((1, d), lambda i, p: (0, 0)),
                      pl.BlockSpec(memory_space=pl.ANY)],
            out_specs=out_specs,
            scratch_shapes=[pltpu.VMEM((TOP_K, TM_ROWS, d), F32), pltpu.SemaphoreType.DMA(())]),
        out_shape=out_shape,
        compiler_params=_params(("arbitrary",)),
        name="combine",
    )(pos_flat, x, meta, g.reshape(1, d), ys)


def _moe_plan(meta, counts, n_tokens, n_col_blocks):
    nt_max = (TOP_K * n_tokens) // TM_MOE + N_EXPERTS
    cnt = counts[0, :N_EXPERTS].astype(jnp.int32)
    tiles = (cnt + TM_MOE - 1) // TM_MOE
    tiles_end = jnp.cumsum(tiles)
    tiles_start = tiles_end - tiles
    n_tiles = tiles_end[-1]

    ids = meta[:, 0:2].astype(jnp.int32)
    rank = meta[:, 2:4].astype(jnp.int32)
    pos = (tiles_start[ids] * TM_MOE + rank).reshape(-1)

    tile_idx = jnp.arange(nt_max, dtype=jnp.int32)
    t_valid = tile_idx < n_tiles
    t_idx = jnp.minimum(tile_idx, n_tiles - 1)
    t_e = jnp.searchsorted(tiles_end, t_idx, side="right").astype(jnp.int32)
    tiles_plan = (t_e, t_idx, t_valid.astype(jnp.int32))

    item_idx = jnp.arange(nt_max * n_col_blocks, dtype=jnp.int32)
    n_items = n_tiles * n_col_blocks
    i_valid = item_idx < n_items
    i_idx = jnp.minimum(item_idx, n_items - 1)
    i_e = jnp.searchsorted(tiles_end * n_col_blocks, i_idx, side="right").astype(jnp.int32)
    local = i_idx - tiles_start[i_e] * n_col_blocks
    te = jnp.maximum(tiles[i_e], 1)
    i_j = local // te
    i_l = local % te
    i_blk = tiles_start[i_e] + i_l
    i_first = jnp.logical_and(i_l == 0, i_valid)
    items_plan = (i_e, i_j.astype(jnp.int32), i_blk.astype(jnp.int32),
                  i_first.astype(jnp.int32), i_valid.astype(jnp.int32))
    return pos, tiles_plan, items_plan, nt_max * TM_MOE


def _moe(x, h, router_w, w1, w3, w2, g_next, emit_x):
    t = x.shape[0]
    meta, counts = _router(h, router_w)
    pos, tiles_plan, items_plan, n_rows = _moe_plan(meta, counts, t, w1.shape[2] // TN_UP)
    xs = _dispatch(pos, h, n_rows)
    a = _gmm_up(items_plan, xs, w1, w3)
    ys = _gmm_down(tiles_plan, a, w2)
    return _combine(pos, x, meta, g_next, ys, emit_x)


def _mix_params(spatial_w, spatial_b, sample_len):
    mask = jnp.tril(jnp.ones((CHUNK, CHUNK), dtype=bool))
    w = jnp.where(mask[None], spatial_w, 0)
    b_prompt = jnp.broadcast_to(spatial_b[:, :, None], (H_C, CHUNK, HEAD_DIM))
    eye = jnp.eye(CHUNK // sample_len, dtype=w.dtype)
    w_small = w[:, :sample_len, :sample_len]
    w_sample = jnp.einsum("ab,hts->hatbs", eye, w_small).reshape(H_C, CHUNK, CHUNK)
    b_small = jnp.tile(spatial_b[:, :sample_len], (1, CHUNK // sample_len))
    b_sample = jnp.broadcast_to(b_small[:, :, None], (H_C, CHUNK, HEAD_DIM))
    return w.astype(BF16), b_prompt, w_sample.astype(BF16), b_sample


def kernel(x_prompt, x_sample, state_conv_a, state_conv_b, norm1_g, norm2_g, final_g, w_in, w_out,
           conv_a_w, conv_a_b, ln_a_g, ln_a_b, conv_b_w, ln_c_g, ln_c_b, spatial_w, spatial_b,
           group_g, ffn_w1, ffn_w3, ffn_w2, router_w, moe_w1, moe_w3, moe_w2):
    n_batch, seq_len, d = x_prompt.shape
    n_dec, dec_len, _ = x_sample.shape
    n_prompt = n_batch * seq_len
    assert dec_len == SUBLANES and d == D_MODEL

    x = jnp.concatenate([x_prompt.reshape(n_prompt, d), x_sample.reshape(n_dec * dec_len, d)], axis=0)
    h = _norm(x, norm1_g[0])

    glus, pres, vs = [], [], []
    for l in range(DEPTH):
        mw_p, mb_p, mw_s, mb_s = _mix_params(spatial_w[l], spatial_b[l], dec_len)
        lw = dict(conv_a_w=conv_a_w[l], conv_a_b=conv_a_b[l].reshape(1, D_A),
                  ln_a_g=ln_a_g[l].reshape(1, D_A), ln_a_b=ln_a_b[l].reshape(1, D_A),
                  conv_b_w=conv_b_w[l], ln_c_g=ln_c_g[l].reshape(1, D_C),
                  ln_c_b=ln_c_b[l].reshape(1, D_C), group_g=group_g[l].reshape(1, D_MODEL),
                  mix_w_prompt=mw_p, mix_b_prompt=mb_p, mix_w_sample=mw_s, mix_b_sample=mb_s)
        z = _mm_up(h, [w_in[l]], F32)
        y, glu, pre, v = _mixers(z, state_conv_a[l], state_conv_b[l], lw, n_prompt, seq_len)
        glus.append(glu)
        pres.append(pre)
        vs.append(v)
        last = l == DEPTH - 1
        g_next = final_g if last else norm1_g[l + 1]
        i = l // 2
        if l % 2 == 0:
            x, h = _mm_down(y, w_out[l], x, norm2_g[l], BF16)
            a = _mm_up(h, [ffn_w1[i], ffn_w3[i]], BF16)
            x, h = _mm_down(a, ffn_w2[i], x, g_next, BF16)
        else:
            x, h = _mm_down(y, w_out[l], x, norm2_g[l], F32)
            outs = _moe(x, h, router_w[i], moe_w1[i], moe_w3[i], moe_w2[i], g_next, not last)
            if last:
                y_final = outs[0]
            else:
                x, h = outs

    def prompt_rows(a):
        return a[:n_prompt].reshape(n_batch, seq_len, a.shape[-1])

    def sample_rows(a):
        return a[n_prompt:].reshape(n_dec, dec_len, a.shape[-1])

    ka, kb = CONV_A_WIDTH - 1, CONV_B_WIDTH - 1
    chunk_start = ((seq_len - 1) // CHUNK) * CHUNK
    pa = jnp.stack([prompt_rows(g)[:, seq_len - ka:] for g in glus])
    pb = jnp.stack([prompt_rows(p)[:, seq_len - kb:] for p in pres])
    pv = jnp.stack([prompt_rows(v)[:, chunk_start:] for v in vs])
    sa = jnp.stack([jnp.concatenate([state_conv_a[l][:, dec_len:], sample_rows(glus[l])], axis=1)
                    for l in range(DEPTH)])
    sb = jnp.stack([sample_rows(p)[:, dec_len - kb:] for p in pres])
    sv = jnp.stack([sample_rows(v) for v in vs])
    return (prompt_rows(y_final), sample_rows(y_final), pa, pb, pv, sa, sb, sv)
```

```python
import functools

import jax
import jax.numpy as jnp
from jax import lax
from jax.experimental import pallas as pl
from jax.experimental.pallas import tpu as pltpu

F32 = jnp.float32
BF16 = jnp.bfloat16

D_MODEL = 2048
DEPTH = 4
HEAD_DIM = 128
D_A = 6 * HEAD_DIM
D_B = 5 * HEAD_DIM
D_C = 5 * HEAD_DIM
H_C = 5
CONV_A_WIDTH = 31
CONV_B_WIDTH = 3
CHUNK = 128
IN_COLS = 2 * D_A + 3 * D_B + 2 * D_C
N_EXPERTS = 8
TOP_K = 2
RMS_EPS = 1e-6
LN_EPS = 1e-5

O_AVAL = 0
O_AGATE = D_A
O_BGATE = 2 * D_A
O_CGATE = 2 * D_A + D_B
O_BIN = 2 * D_A + 2 * D_B
O_CU = 2 * D_A + 3 * D_B
O_CV = 2 * D_A + 3 * D_B + D_C

LANES = 128
SUBLANES = 8
VMEM_LIMIT_BYTES = 56 * 1024 * 1024

TM_UP = 1024
TN_UP1 = 1024
TN_UP2 = 512
TM_DOWN = 512
TK_DOWN = 512
TK_DOWN_WIDE = 1024
TM_MOE = 1024
TM_ROUTER = 512
TM_ROWS = 256
HALO = 32


def _params(sem):
    return pltpu.CompilerParams(dimension_semantics=sem, vmem_limit_bytes=VMEM_LIMIT_BYTES)


def _rms(x):
    return x * lax.rsqrt(jnp.mean(x * x, axis=-1, keepdims=True) + RMS_EPS)


def _ln(x, g, b):
    mu = jnp.mean(x, axis=-1, keepdims=True)
    xc = x - mu
    var = jnp.mean(xc * xc, axis=-1, keepdims=True)
    return xc * lax.rsqrt(var + LN_EPS) * g + b


def _full(shape):
    nd = len(shape)
    return pl.BlockSpec(shape, lambda *_: (0,) * nd)


def _tk_down(kdim):
    return TK_DOWN_WIDE if kdim % TK_DOWN_WIDE == 0 else TK_DOWN


def _norm_body(x_ref, g_ref, o_ref):
    o_ref[...] = (_rms(x_ref[...]) * g_ref[...]).astype(o_ref.dtype)


def _norm(x, g):
    t, d = x.shape
    return pl.pallas_call(
        _norm_body,
        grid=(t // TM_DOWN,),
        in_specs=[pl.BlockSpec((TM_DOWN, d), lambda i: (i, 0)),
                  pl.BlockSpec((1, d), lambda i: (0, 0))],
        out_specs=pl.BlockSpec((TM_DOWN, d), lambda i: (i, 0)),
        out_shape=jax.ShapeDtypeStruct((t, d), BF16),
        compiler_params=_params(("arbitrary",)),
        name="norm_in",
    )(x, g.reshape(1, d))


def _up_compute(x_ref, wbf, o_ref, n_w):
    x = x_ref[...]
    a = jnp.dot(x, wbf[0], preferred_element_type=F32)
    if n_w == 2:
        b = jnp.dot(x, wbf[1], preferred_element_type=F32)
        a = jax.nn.silu(a) * b
    o_ref[...] = a.astype(o_ref.dtype)


def _mm_up_body(x_ref, *refs, n_w):
    w_refs, o_ref, wbf = refs[:n_w], refs[n_w], refs[n_w + 1]

    @pl.when(pl.program_id(1) == 0)
    def _():
        for n in range(n_w):
            wbf[n] = w_refs[n][...].astype(BF16)

    _up_compute(x_ref, wbf, o_ref, n_w)


def _mm_up(x, ws, layer, out_dtype):
    t, k = x.shape
    n = ws[0].shape[2]
    n_w = len(ws)
    tn = TN_UP1 if n_w == 1 else TN_UP2
    return pl.pallas_call(
        functools.partial(_mm_up_body, n_w=n_w),
        grid=(pl.cdiv(n, tn), t // TM_UP),
        in_specs=[pl.BlockSpec((TM_UP, k), lambda j, i: (i, 0))]
        + [pl.BlockSpec((None, k, tn), lambda j, i: (layer, 0, j)) for _ in ws],
        out_specs=pl.BlockSpec((TM_UP, tn), lambda j, i: (i, j)),
        out_shape=jax.ShapeDtypeStruct((t, n), out_dtype),
        scratch_shapes=[pltpu.VMEM((n_w, k, tn), BF16)],
        compiler_params=_params(("arbitrary", "arbitrary")),
        name="mm_up%d" % n_w,
    )(x, *ws)


def _mm_down_body(a_ref, w_ref, x_ref, g_ref, xo_ref, ho_ref, *, nk):
    k = pl.program_id(1)

    def partial_product():
        return jnp.dot(a_ref[...], w_ref[...].astype(BF16), preferred_element_type=F32)

    @pl.when(k == 0)
    def _():
        xo_ref[...] = partial_product() + x_ref[...]

    @pl.when(k > 0)
    def _():
        xo_ref[...] = partial_product() + xo_ref[...]

    @pl.when(k == nk - 1)
    def _():
        ho_ref[...] = (_rms(xo_ref[...]) * g_ref[...]).astype(ho_ref.dtype)


def _mm_down(a, w, layer, x, g, h_dtype):
    t, kdim = a.shape
    d = w.shape[2]
    tk = _tk_down(kdim)
    nk = kdim // tk
    return pl.pallas_call(
        functools.partial(_mm_down_body, nk=nk),
        grid=(t // TM_DOWN, nk),
        in_specs=[pl.BlockSpec((TM_DOWN, tk), lambda i, k: (i, k)),
                  pl.BlockSpec((None, tk, d), lambda i, k: (layer, k, 0)),
                  pl.BlockSpec((TM_DOWN, d), lambda i, k: (i, 0)),
                  pl.BlockSpec((1, d), lambda i, k: (0, 0))],
        out_specs=[pl.BlockSpec((TM_DOWN, d), lambda i, k: (i, 0)),
                   pl.BlockSpec((TM_DOWN, d), lambda i, k: (i, 0))],
        out_shape=[jax.ShapeDtypeStruct((t, d), F32), jax.ShapeDtypeStruct((t, d), h_dtype)],
        compiler_params=_params(("arbitrary", "arbitrary")),
        name="mm_down",
    )(a, w, x, g.reshape(1, d))


def _mixer_tail(ya, yb, z_ref, p, mixw_ref, mixb_ref, y_ref, v_ref):
    rows = ya.shape[0]
    ya = jax.nn.silu(_ln(ya + p["cab"][...], p["lag"][...], p["lab"][...]))
    gg = p["gg"]
    y_ref[:, 0:D_A] = (_rms(ya) * gg[:, 0:D_A]).astype(y_ref.dtype)
    y_ref[:, D_A:D_A + D_B] = (_rms(yb) * gg[:, D_A:D_A + D_B]).astype(y_ref.dtype)

    u = jax.nn.gelu(z_ref[:, O_CU:O_CU + D_C])
    v = _ln(jax.nn.gelu(z_ref[:, O_CV:O_CV + D_C]), p["lcg"][...], p["lcb"][...])
    v_ref[...] = v
    vb = v.astype(BF16)
    for c in range(rows // CHUNK):
        r0 = c * CHUNK
        heads = []
        for h in range(H_C):
            l0 = h * HEAD_DIM
            mixed = jnp.dot(mixw_ref[h], vb[r0:r0 + CHUNK, l0:l0 + HEAD_DIM],
                            preferred_element_type=F32) + mixb_ref[h]
            heads.append(u[r0:r0 + CHUNK, l0:l0 + HEAD_DIM] * mixed)
        yc = jnp.concatenate(heads, axis=1)
        y_ref[r0:r0 + CHUNK, D_A + D_B:D_MODEL] = (
            _rms(yc) * gg[:, D_A + D_B:D_MODEL]).astype(y_ref.dtype)


def _mixer_prompt(z_ref, zh_ref, p, has_prev, y_ref, glu_ref, pre_ref, v_ref, exta, extb):
    tq = z_ref.shape[0]
    caw, cbw = p["caw"], p["cbw"]

    glu = z_ref[:, O_AVAL:O_AVAL + D_A] * jax.nn.sigmoid(z_ref[:, O_AGATE:O_AGATE + D_A])
    glu_ref[...] = glu
    exta[0:HALO, :] = jnp.where(has_prev, zh_ref[:, O_AVAL:O_AVAL + D_A]
                                * jax.nn.sigmoid(zh_ref[:, O_AGATE:O_AGATE + D_A]), 0.0)
    exta[HALO:HALO + tq, :] = glu
    off = HALO - (CONV_A_WIDTH - 1)
    ya = caw[0:1, :] * exta[off:off + tq, :]
    for k in range(1, CONV_A_WIDTH):
        ya = ya + caw[k:k + 1, :] * exta[off + k:off + k + tq, :]

    pre = z_ref[:, O_CGATE:O_CGATE + D_B] * z_ref[:, O_BIN:O_BIN + D_B]
    pre_ref[...] = pre
    extb[0:SUBLANES, :] = jnp.where(has_prev, zh_ref[HALO - SUBLANES:HALO, O_CGATE:O_CGATE + D_B]
                                    * zh_ref[HALO - SUBLANES:HALO, O_BIN:O_BIN + D_B], 0.0)
    extb[SUBLANES:SUBLANES + tq, :] = pre
    off = SUBLANES - (CONV_B_WIDTH - 1)
    yb = cbw[0:1, :] * extb[off:off + tq, :]
    for k in range(1, CONV_B_WIDTH):
        yb = yb + cbw[k:k + 1, :] * extb[off + k:off + k + tq, :]
    yb = z_ref[:, O_BGATE:O_BGATE + D_B] * yb

    _mixer_tail(ya, yb, z_ref, p, p["mixw_p"], p["mixb_p"], y_ref, v_ref)


def _shift_rows_in_seq(x, d, row_in_seq):
    if d == 0:
        return x
    return jnp.where(row_in_seq >= d, pltpu.roll(x, d, axis=0), 0.0)


def _mixer_sample(z_ref, ha_ref, hb_ref, p, y_ref, glu_ref, pre_ref, v_ref, hista):
    rows = z_ref.shape[0]
    nseq, seq = rows // SUBLANES, SUBLANES
    hist_len = CONV_A_WIDTH - 1
    caw, cbw = p["caw"], p["cbw"]

    glu = z_ref[:, O_AVAL:O_AVAL + D_A] * jax.nn.sigmoid(z_ref[:, O_AGATE:O_AGATE + D_A])
    glu_ref[...] = glu
    hista[...] = jnp.zeros_like(hista)
    hista[:, 0:hist_len, :] = ha_ref[...]
    row_a = lax.broadcasted_iota(jnp.int32, (rows, D_A), 0) % seq
    ya = caw[hist_len:hist_len + 1, :] * glu
    for d in range(1, seq):
        ya = ya + caw[hist_len - d:hist_len - d + 1, :] * _shift_rows_in_seq(glu, d, row_a)
    for k in range(hist_len):
        ya = ya + caw[k:k + 1, :] * hista[:, k:k + seq, :].reshape(rows, D_A)

    pre = z_ref[:, O_CGATE:O_CGATE + D_B] * z_ref[:, O_BIN:O_BIN + D_B]
    pre_ref[...] = pre
    row_b = lax.broadcasted_iota(jnp.int32, (rows, D_B), 0) % seq
    h0 = jnp.broadcast_to(hb_ref[:, 0:1, :], (nseq, seq, D_B)).reshape(rows, D_B)
    h1 = jnp.broadcast_to(hb_ref[:, 1:2, :], (nseq, seq, D_B)).reshape(rows, D_B)
    x1 = jnp.where(row_b == 0, h1, _shift_rows_in_seq(pre, 1, row_b))
    x0 = jnp.where(row_b == 0, h0, jnp.where(row_b == 1, h1, _shift_rows_in_seq(pre, 2, row_b)))
    yb = cbw[0:1, :] * x0 + cbw[1:2, :] * x1 + cbw[2:3, :] * pre
    yb = z_ref[:, O_BGATE:O_BGATE + D_B] * yb

    _mixer_tail(ya, yb, z_ref, p, p["mixw_s"], p["mixb_s"], y_ref, v_ref)


_MIXER_PARAMS = ("caw", "cab", "lag", "lab", "cbw", "lcg", "lcb", "gg",
                 "mixw_p", "mixb_p", "mixw_s", "mixb_s")


def _mixer_body(z_ref, zh_ref, ha_ref, hb_ref, *refs, n_prompt_blocks, blocks_per_seq):
    n_p = len(_MIXER_PARAMS)
    p = dict(zip(_MIXER_PARAMS, refs[:n_p]))
    y_ref, glu_ref, pre_ref, v_ref, exta, extb, hista = refs[n_p:]
    i = pl.program_id(0)

    @pl.when(i < n_prompt_blocks)
    def _():
        has_prev = lax.rem(i, blocks_per_seq) > 0
        _mixer_prompt(z_ref, zh_ref, p, has_prev, y_ref, glu_ref, pre_ref, v_ref, exta, extb)

    @pl.when(i >= n_prompt_blocks)
    def _():
        _mixer_sample(z_ref, ha_ref, hb_ref, p, y_ref, glu_ref, pre_ref, v_ref, hista)


def _mixers(z, hist_a, hist_b, layer, lw, n_prompt, seq_len):
    t = z.shape[0]
    tq = TM_ROWS
    n_prompt_blocks = n_prompt // tq
    nseq_blk = tq // SUBLANES
    halo_per_block = tq // HALO
    widths = (D_MODEL, D_A, D_B, D_C)
    dtypes = (BF16, F32, F32, F32)

    def sample_block(i):
        return jnp.maximum(i - n_prompt_blocks, 0)

    param_specs = [_full((CONV_A_WIDTH, D_A)), _full((1, D_A)), _full((1, D_A)), _full((1, D_A)),
                   _full((CONV_B_WIDTH, D_B)), _full((1, D_C)), _full((1, D_C)),
                   _full((1, D_MODEL))] + [_full((H_C, CHUNK, CHUNK))] * 4
    return pl.pallas_call(
        functools.partial(_mixer_body, n_prompt_blocks=n_prompt_blocks,
                          blocks_per_seq=seq_len // tq),
        grid=(t // tq,),
        in_specs=[pl.BlockSpec((tq, IN_COLS), lambda i: (i, 0)),
                  pl.BlockSpec((HALO, IN_COLS),
                               lambda i: (jnp.maximum(i * halo_per_block - 1, 0), 0)),
                  pl.BlockSpec((None, nseq_blk, CONV_A_WIDTH - 1, D_A),
                               lambda i: (layer, sample_block(i), 0, 0)),
                  pl.BlockSpec((None, nseq_blk, CONV_B_WIDTH - 1, D_B),
                               lambda i: (layer, sample_block(i), 0, 0))] + param_specs,
        out_specs=[pl.BlockSpec((tq, w), lambda i: (i, 0)) for w in widths],
        out_shape=[jax.ShapeDtypeStruct((t, w), dt) for w, dt in zip(widths, dtypes)],
        scratch_shapes=[pltpu.VMEM((HALO + tq, D_A), F32), pltpu.VMEM((SUBLANES + tq, D_B), F32),
                        pltpu.VMEM((nseq_blk, CONV_A_WIDTH - 1 + SUBLANES + 2, D_A), F32)],
        compiler_params=_params(("arbitrary",)),
        name="mixer",
    )(z, z, hist_a, hist_b, *[lw[name] for name in _MIXER_PARAMS])


def _router_body(h_ref, rw_ref, meta_ref, cnt_ref, carry):
    @pl.when(pl.program_id(0) == 0)
    def _():
        carry[...] = jnp.zeros_like(carry)

    tm = h_ref.shape[0]
    logits = jnp.dot(h_ref[...], rw_ref[...], precision=lax.Precision.HIGHEST,
                     preferred_element_type=F32)
    lane = lax.broadcasted_iota(jnp.int32, (tm, LANES), 1).astype(F32)
    neg = -jnp.inf
    l1 = jnp.where(lane < N_EXPERTS, logits, neg)
    m1 = jnp.max(l1, axis=1, keepdims=True)
    i1 = jnp.min(jnp.where(l1 == m1, lane, float(LANES)), axis=1, keepdims=True)
    l2 = jnp.where(lane == i1, neg, l1)
    m2 = jnp.max(l2, axis=1, keepdims=True)
    i2 = jnp.min(jnp.where(l2 == m2, lane, float(LANES)), axis=1, keepdims=True)
    e = jnp.exp(m2 - m1)
    den = 1.0 + e
    g1 = 1.0 / den
    g2 = e / den

    oh1 = lane == i1
    oh2 = lane == i2
    onehot = jnp.where(oh1, 1.0, jnp.where(oh2, 1.0, 0.0))
    r = lax.broadcasted_iota(jnp.int32, (tm, tm), 0)
    c = lax.broadcasted_iota(jnp.int32, (tm, tm), 1)
    earlier = jnp.where(c < r, 1.0, 0.0).astype(BF16)
    before = jnp.dot(earlier, onehot.astype(BF16), preferred_element_type=F32) + carry[...]
    r1 = jnp.sum(jnp.where(oh1, before, 0.0), axis=1, keepdims=True)
    r2 = jnp.sum(jnp.where(oh2, before, 0.0), axis=1, keepdims=True)
    carry[...] += jnp.sum(onehot, axis=0, keepdims=True)
    cnt_ref[...] = carry[...]

    meta = jnp.where(lane == 0, i1, jnp.where(lane == 1, i2, jnp.where(lane == 2, r1, jnp.where(
        lane == 3, r2, jnp.where(lane == 4, g1, jnp.where(lane == 5, g2, 0.0))))))
    meta_ref[...] = meta


def _router(h, router_w):
    t, d = h.shape
    rw = jnp.pad(router_w, ((0, 0), (0, LANES - N_EXPERTS)))
    return pl.pallas_call(
        _router_body,
        grid=(t // TM_ROUTER,),
        in_specs=[pl.BlockSpec((TM_ROUTER, d), lambda i: (i, 0)), _full((d, LANES))],
        out_specs=[pl.BlockSpec((TM_ROUTER, LANES), lambda i: (i, 0)), _full((1, LANES))],
        out_shape=[jax.ShapeDtypeStruct((t, LANES), F32), jax.ShapeDtypeStruct((1, LANES), F32)],
        scratch_shapes=[pltpu.VMEM((1, LANES), F32)],
        compiler_params=_params(("arbitrary",)),
        name="router",
    )(h, rw)


def _gather_rows(src_ref, row_of, dst_ref, sem, n):
    def copy(r):
        return pltpu.make_async_copy(src_ref.at[pl.ds(row_of(r), 1), :],
                                     dst_ref.at[pl.ds(r, 1), :], sem)

    def issue(r, carry):
        copy(r).start()
        return carry

    def wait(r, carry):
        copy(r).wait()
        return carry

    lax.fori_loop(0, n, issue, 0)
    lax.fori_loop(0, n, wait, 0)


def _dispatch_body(tok, blk_valid, h_ref, o_ref, buf, sem):
    n = pl.program_id(0)
    base = n * TM_ROWS

    @pl.when(blk_valid[n] == 1)
    def _():
        _gather_rows(h_ref, lambda r: tok[base + r], buf, sem, TM_ROWS)
        o_ref[...] = buf[...].astype(o_ref.dtype)

    @pl.when(blk_valid[n] == 0)
    def _():
        o_ref[...] = jnp.zeros_like(o_ref)


def _dispatch(tok_of_row, blk_valid, h):
    n_rows = tok_of_row.shape[0]
    d = h.shape[1]
    return pl.pallas_call(
        _dispatch_body,
        grid_spec=pltpu.PrefetchScalarGridSpec(
            num_scalar_prefetch=2,
            grid=(n_rows // TM_ROWS,),
            in_specs=[pl.BlockSpec(memory_space=pl.ANY)],
            out_specs=pl.BlockSpec((TM_ROWS, d), lambda n, tok, v: (n, 0)),
            scratch_shapes=[pltpu.VMEM((TM_ROWS, d), F32), pltpu.SemaphoreType.DMA(())]),
        out_shape=jax.ShapeDtypeStruct((n_rows, d), BF16),
        compiler_params=_params(("arbitrary",)),
        name="dispatch",
    )(tok_of_row, blk_valid, h)


def _combine_body(pos, x_ref, meta_ref, g_ref, ys_ref, *refs, emit_x):
    out_refs, ybuf, sem = refs[:-2], refs[-2], refs[-1]
    base = pl.program_id(0) * TM_ROWS
    for k in range(TOP_K):
        _gather_rows(ys_ref, lambda r, k=k: pos[TOP_K * (base + r) + k], ybuf.at[k], sem, TM_ROWS)

    xn = x_ref[...] + (meta_ref[:, 4:5] * ybuf[0] + meta_ref[:, 5:6] * ybuf[1])
    h = _rms(xn) * g_ref[...]
    if emit_x:
        out_refs[0][...] = xn
        out_refs[1][...] = h.astype(out_refs[1].dtype)
    else:
        out_refs[0][...] = h.astype(out_refs[0].dtype)


def _combine(pos_flat, x, meta, g, ys, emit_x):
    t, d = x.shape
    blk = pl.BlockSpec((TM_ROWS, d), lambda i, p: (i, 0))
    if emit_x:
        out_specs = [blk, blk]
        out_shape = [jax.ShapeDtypeStruct((t, d), F32), jax.ShapeDtypeStruct((t, d), BF16)]
    else:
        out_specs = [blk]
        out_shape = [jax.ShapeDtypeStruct((t, d), F32)]
    return pl.pallas_call(
        functools.partial(_combine_body, emit_x=emit_x),
        grid_spec=pltpu.PrefetchScalarGridSpec(
            num_scalar_prefetch=1,
            grid=(t // TM_ROWS,),
            in_specs=[blk, pl.BlockSpec((TM_ROWS, LANES), lambda i, p: (i, 0)),
                      pl.BlockSpec((1, d), lambda i, p: (0, 0)),
                      pl.BlockSpec(memory_space=pl.ANY)],
            out_specs=out_specs,
            scratch_shapes=[pltpu.VMEM((TOP_K, TM_ROWS, d), F32), pltpu.SemaphoreType.DMA(())]),
        out_shape=out_shape,
        compiler_params=_params(("arbitrary",)),
        name="combine",
    )(pos_flat, x, meta, g.reshape(1, d), ys)


def _gmm_up_body(it_e, it_jw, it_xblk, it_oblk, it_jo, it_first, it_valid,
                 x_ref, w1_ref, w3_ref, o_ref, wbf):
    del it_e, it_jw, it_xblk, it_oblk, it_jo
    n = pl.program_id(0)

    @pl.when(it_valid[n] == 1)
    def _():
        @pl.when(it_first[n] == 1)
        def _():
            wbf[0] = w1_ref[...].astype(BF16)
            wbf[1] = w3_ref[...].astype(BF16)

        _up_compute(x_ref, wbf, o_ref, 2)

    @pl.when(it_valid[n] == 0)
    def _():
        o_ref[...] = jnp.zeros_like(o_ref)


def _gmm_up(items, xs, w1, w3, layer):
    n_rows, k = xs.shape
    n_ff = w1.shape[3]
    n_items = items[0].shape[0]

    def w_map(n, e, jw, xb, ob, jo, f, v):
        return (layer, e[n], 0, jw[n])

    return pl.pallas_call(
        _gmm_up_body,
        grid_spec=pltpu.PrefetchScalarGridSpec(
            num_scalar_prefetch=7,
            grid=(n_items,),
            in_specs=[pl.BlockSpec((TM_MOE, k), lambda n, e, jw, xb, ob, jo, f, v: (xb[n], 0)),
                      pl.BlockSpec((None, None, k, TN_UP2), w_map),
                      pl.BlockSpec((None, None, k, TN_UP2), w_map)],
            out_specs=pl.BlockSpec((TM_MOE, TN_UP2),
                                   lambda n, e, jw, xb, ob, jo, f, v: (ob[n], jo[n])),
            scratch_shapes=[pltpu.VMEM((2, k, TN_UP2), BF16)]),
        out_shape=jax.ShapeDtypeStruct((n_rows, n_ff), BF16),
        compiler_params=_params(("arbitrary",)),
        name="gmm_up",
    )(*items, xs, w1, w3)


def _gmm_down_body(t_e, t_ablk, t_valid, a_ref, w_ref, o_ref):
    del t_e, t_ablk
    n = pl.program_id(0)
    k = pl.program_id(1)
    valid = t_valid[n] == 1

    def partial_product():
        return jnp.dot(a_ref[...], w_ref[...].astype(BF16), preferred_element_type=F32)

    @pl.when(jnp.logical_and(valid, k == 0))
    def _():
        o_ref[...] = partial_product()

    @pl.when(jnp.logical_and(valid, k > 0))
    def _():
        o_ref[...] = partial_product() + o_ref[...]

    @pl.when(jnp.logical_and(jnp.logical_not(valid), k == 0))
    def _():
        o_ref[...] = jnp.zeros_like(o_ref)


def _gmm_down(tiles, a, w2, layer):
    n_rows, n_ff = a.shape
    d = w2.shape[3]
    tk = _tk_down(n_ff)
    nk = n_ff // tk
    n_tiles = tiles[0].shape[0]

    def kk(n, k, v):
        return jnp.where(v[n] == 1, k, nk - 1)

    return pl.pallas_call(
        _gmm_down_body,
        grid_spec=pltpu.PrefetchScalarGridSpec(
            num_scalar_prefetch=3,
            grid=(n_tiles, nk),
            in_specs=[pl.BlockSpec((TM_MOE, tk), lambda n, k, e, b, v: (b[n], kk(n, k, v))),
                      pl.BlockSpec((None, None, tk, d),
                                   lambda n, k, e, b, v: (layer, e[n], kk(n, k, v), 0))],
            out_specs=pl.BlockSpec((TM_MOE, d), lambda n, k, e, b, v: (n, 0))),
        out_shape=jax.ShapeDtypeStruct((n_rows, d), F32),
        compiler_params=_params(("arbitrary", "arbitrary")),
        name="gmm_down",
    )(*tiles, a, w2)


def _moe_plan(meta, counts, n_tokens, n_col_blocks):
    nt_max = (TOP_K * n_tokens) // TM_MOE + N_EXPERTS
    n_rows = nt_max * TM_MOE
    cnt = counts[0, :N_EXPERTS].astype(jnp.int32)
    tiles = (cnt + TM_MOE - 1) // TM_MOE
    tiles_end = jnp.cumsum(tiles)
    tiles_start = tiles_end - tiles
    n_tiles = tiles_end[-1]

    ids = meta[:, 0:2].astype(jnp.int32)
    rank = meta[:, 2:4].astype(jnp.int32)
    pos = (tiles_start[ids] * TM_MOE + rank).reshape(-1)
    token = jnp.repeat(jnp.arange(n_tokens, dtype=jnp.int32), TOP_K)
    tok_of_row = jnp.zeros((n_rows,), jnp.int32).at[pos].set(token, unique_indices=True)
    blk_valid = (jnp.arange(n_rows // TM_ROWS, dtype=jnp.int32)
                 < n_tiles * (TM_MOE // TM_ROWS)).astype(jnp.int32)

    tile_idx = jnp.arange(nt_max, dtype=jnp.int32)
    t_valid = tile_idx < n_tiles
    t_idx = jnp.minimum(tile_idx, n_tiles - 1)
    t_e = jnp.searchsorted(tiles_end, t_idx, side="right").astype(jnp.int32)
    tiles_plan = (t_e, t_idx, t_valid.astype(jnp.int32))

    item_idx = jnp.arange(nt_max * n_col_blocks, dtype=jnp.int32)
    n_items = n_tiles * n_col_blocks
    i_valid = item_idx < n_items
    i_idx = jnp.minimum(item_idx, n_items - 1)
    i_e = jnp.searchsorted(tiles_end * n_col_blocks, i_idx, side="right").astype(jnp.int32)
    local = i_idx - tiles_start[i_e] * n_col_blocks
    te = jnp.maximum(tiles[i_e], 1)
    i_j = (local // te).astype(jnp.int32)
    i_l = local % te
    i_blk = (tiles_start[i_e] + i_l).astype(jnp.int32)
    i_first = jnp.logical_and(i_l == 0, i_valid)
    tail = jnp.maximum(item_idx - n_items, 0)
    o_blk = jnp.where(i_valid, i_blk, n_tiles + tail // n_col_blocks).astype(jnp.int32)
    o_j = jnp.where(i_valid, i_j, tail % n_col_blocks).astype(jnp.int32)
    items_plan = (i_e, i_j, i_blk, o_blk, o_j, i_first.astype(jnp.int32),
                  i_valid.astype(jnp.int32))
    return pos, tok_of_row, blk_valid, tiles_plan, items_plan


def _moe(x, h, router_w, w1, w3, w2, layer, g_next, emit_x):
    t = x.shape[0]
    meta, counts = _router(h, router_w)
    pos, tok_of_row, blk_valid, tiles_plan, items_plan = _moe_plan(
        meta, counts, t, w1.shape[3] // TN_UP2)
    xs = _dispatch(tok_of_row, blk_valid, h)
    a = _gmm_up(items_plan, xs, w1, w3, layer)
    ys = _gmm_down(tiles_plan, a, w2, layer)
    return _combine(pos, x, meta, g_next, ys, emit_x)


def _mix_params(spatial_w, spatial_b, sample_len):
    mask = jnp.tril(jnp.ones((CHUNK, CHUNK), dtype=bool))
    w = jnp.where(mask[None], spatial_w, 0)
    b_prompt = jnp.broadcast_to(spatial_b[:, :, None], (H_C, CHUNK, HEAD_DIM))
    eye = jnp.eye(CHUNK // sample_len, dtype=w.dtype)
    w_small = w[:, :sample_len, :sample_len]
    w_sample = jnp.einsum("ab,hts->hatbs", eye, w_small).reshape(H_C, CHUNK, CHUNK)
    b_small = jnp.tile(spatial_b[:, :sample_len], (1, CHUNK // sample_len))
    b_sample = jnp.broadcast_to(b_small[:, :, None], (H_C, CHUNK, HEAD_DIM))
    return w.astype(BF16), b_prompt, w_sample.astype(BF16), b_sample


def kernel(x_prompt, x_sample, state_conv_a, state_conv_b, norm1_g, norm2_g, final_g, w_in, w_out,
           conv_a_w, conv_a_b, ln_a_g, ln_a_b, conv_b_w, ln_c_g, ln_c_b, spatial_w, spatial_b,
           group_g, ffn_w1, ffn_w3, ffn_w2, router_w, moe_w1, moe_w3, moe_w2):
    n_batch, seq_len, d = x_prompt.shape
    n_dec, dec_len, _ = x_sample.shape
    n_prompt = n_batch * seq_len
    assert dec_len == SUBLANES and d == D_MODEL

    x = jnp.concatenate([x_prompt.reshape(n_prompt, d), x_sample.reshape(n_dec * dec_len, d)], axis=0)
    h = _norm(x, norm1_g[0])

    glus, pres, vs = [], [], []
    for l in range(DEPTH):
        mw_p, mb_p, mw_s, mb_s = _mix_params(spatial_w[l], spatial_b[l], dec_len)
        lw = dict(caw=conv_a_w[l], cab=conv_a_b[l].reshape(1, D_A),
                  lag=ln_a_g[l].reshape(1, D_A), lab=ln_a_b[l].reshape(1, D_A),
                  cbw=conv_b_w[l], lcg=ln_c_g[l].reshape(1, D_C),
                  lcb=ln_c_b[l].reshape(1, D_C), gg=group_g[l].reshape(1, D_MODEL),
                  mixw_p=mw_p, mixb_p=mb_p, mixw_s=mw_s, mixb_s=mb_s)
        z = _mm_up(h, [w_in], l, F32)
        y, glu, pre, v = _mixers(z, state_conv_a, state_conv_b, l, lw, n_prompt, seq_len)
        glus.append(glu)
        pres.append(pre)
        vs.append(v)
        last = l == DEPTH - 1
        g_next = final_g if last else norm1_g[l + 1]
        i = l // 2
        if l % 2 == 0:
            x, h = _mm_down(y, w_out, l, x, norm2_g[l], BF16)
            a = _mm_up(h, [ffn_w1, ffn_w3], i, BF16)
            x, h = _mm_down(a, ffn_w2, i, x, g_next, BF16)
        else:
            x, h = _mm_down(y, w_out, l, x, norm2_g[l], F32)
            outs = _moe(x, h, router_w[i], moe_w1, moe_w3, moe_w2, i, g_next, not last)
            if last:
                y_final = outs[0]
            else:
                x, h = outs

    def prompt_rows(a):
        return a[:n_prompt].reshape(n_batch, seq_len, a.shape[-1])

    def sample_rows(a):
        return a[n_prompt:].reshape(n_dec, dec_len, a.shape[-1])

    ka, kb = CONV_A_WIDTH - 1, CONV_B_WIDTH - 1
    chunk_start = ((seq_len - 1) // CHUNK) * CHUNK
    pa = jnp.stack([prompt_rows(g)[:, seq_len - ka:] for g in glus])
    pb = jnp.stack([prompt_rows(p)[:, seq_len - kb:] for p in pres])
    pv = jnp.stack([prompt_rows(v)[:, chunk_start:] for v in vs])
    sa = jnp.stack([jnp.concatenate([state_conv_a[l][:, dec_len:], sample_rows(glus[l])], axis=1)
                    for l in range(DEPTH)])
    sb = jnp.stack([sample_rows(p)[:, dec_len - kb:] for p in pres])
    sv = jnp.stack([sample_rows(v) for v in vs])
    return (prompt_rows(y_final), sample_rows(y_final), pa, pb, pv, sa, sb, sv)
```

```python
import functools

import jax
import jax.numpy as jnp
from jax import lax
from jax.experimental import pallas as pl
from jax.experimental.pallas import tpu as pltpu

F32 = jnp.float32
BF16 = jnp.bfloat16

D_MODEL = 2048
DEPTH = 4
HEAD_DIM = 128
D_A = 6 * HEAD_DIM
D_B = 5 * HEAD_DIM
D_C = 5 * HEAD_DIM
H_C = 5
CONV_A_WIDTH = 31
CONV_B_WIDTH = 3
CHUNK = 128
IN_COLS = 2 * D_A + 3 * D_B + 2 * D_C
N_EXPERTS = 8
TOP_K = 2
RMS_EPS = 1e-6
LN_EPS = 1e-5

O_AVAL = 0
O_AGATE = D_A
O_BGATE = 2 * D_A
O_CGATE = 2 * D_A + D_B
O_BIN = 2 * D_A + 2 * D_B
O_CU = 2 * D_A + 3 * D_B
O_CV = 2 * D_A + 3 * D_B + D_C

LANES = 128
SUBLANES = 8
VMEM_LIMIT_BYTES = 56 * 1024 * 1024

TM_UP = 1024
TN_UP1 = 1024
TN_UP2 = 512
TM_DOWN = 768
TK_DOWN = 512
TK_MOE = 1024
TM_MOE = 1024
TM_SUB = 256
TM_ROUTER = 512
ROW_UNROLL = 8
TM_ROWS = 256
HALO = 32


def _params(sem):
    return pltpu.CompilerParams(dimension_semantics=sem, vmem_limit_bytes=VMEM_LIMIT_BYTES)


def _rms(x):
    return x * lax.rsqrt(jnp.mean(x * x, axis=-1, keepdims=True) + RMS_EPS)


def _ln(x, g, b):
    mu = jnp.mean(x, axis=-1, keepdims=True)
    xc = x - mu
    var = jnp.mean(xc * xc, axis=-1, keepdims=True)
    return xc * lax.rsqrt(var + LN_EPS) * g + b


def _full(shape):
    nd = len(shape)
    return pl.BlockSpec(shape, lambda *_: (0,) * nd)


def _norm_body(x_ref, g_ref, o_ref):
    o_ref[...] = (_rms(x_ref[...]) * g_ref[...]).astype(o_ref.dtype)


def _norm(x, g):
    t, d = x.shape
    return pl.pallas_call(
        _norm_body,
        grid=(t // TM_DOWN,),
        in_specs=[pl.BlockSpec((TM_DOWN, d), lambda i: (i, 0)),
                  pl.BlockSpec((1, d), lambda i: (0, 0))],
        out_specs=pl.BlockSpec((TM_DOWN, d), lambda i: (i, 0)),
        out_shape=jax.ShapeDtypeStruct((t, d), BF16),
        compiler_params=_params(("arbitrary",)),
        name="norm_in",
    )(x, g.reshape(1, d))


def _up_compute(x_ref, wbf, o_ref, n_w):
    x = x_ref[...]
    a = jnp.dot(x, wbf[0], preferred_element_type=F32)
    if n_w == 2:
        b = jnp.dot(x, wbf[1], preferred_element_type=F32)
        a = jax.nn.silu(a) * b
    o_ref[...] = a.astype(o_ref.dtype)


def _mm_up_body(x_ref, *refs, n_w):
    w_refs, o_ref, wbf = refs[:n_w], refs[n_w], refs[n_w + 1]

    @pl.when(pl.program_id(1) == 0)
    def _():
        for n in range(n_w):
            wbf[n] = w_refs[n][...].astype(BF16)

    _up_compute(x_ref, wbf, o_ref, n_w)


def _mm_up(x, ws, layer, out_dtype):
    t, k = x.shape
    n = ws[0].shape[2]
    n_w = len(ws)
    tn = TN_UP1 if n_w == 1 else TN_UP2
    return pl.pallas_call(
        functools.partial(_mm_up_body, n_w=n_w),
        grid=(pl.cdiv(n, tn), t // TM_UP),
        in_specs=[pl.BlockSpec((TM_UP, k), lambda j, i: (i, 0))]
        + [pl.BlockSpec((None, k, tn), lambda j, i: (layer, 0, j)) for _ in ws],
        out_specs=pl.BlockSpec((TM_UP, tn), lambda j, i: (i, j)),
        out_shape=jax.ShapeDtypeStruct((t, n), out_dtype),
        scratch_shapes=[pltpu.VMEM((n_w, k, tn), BF16)],
        compiler_params=_params(("arbitrary", "arbitrary")),
        name="mm_up%d" % n_w,
    )(x, *ws)


def _mm_down_body(a_ref, w_ref, x_ref, *refs, nk, emit_h):
    xo_ref = refs[-2] if emit_h else refs[-1]
    k = pl.program_id(1)

    def partial_product():
        return jnp.dot(a_ref[...], w_ref[...].astype(BF16), preferred_element_type=F32)

    @pl.when(k == 0)
    def _():
        xo_ref[...] = partial_product() + x_ref[...]

    @pl.when(k > 0)
    def _():
        xo_ref[...] = partial_product() + xo_ref[...]

    if emit_h:
        g_ref, ho_ref = refs[0], refs[-1]

        @pl.when(k == nk - 1)
        def _():
            ho_ref[...] = (_rms(xo_ref[...]) * g_ref[...]).astype(ho_ref.dtype)


def _mm_down(a, w, layer, x, g=None):
    t, kdim = a.shape
    d = w.shape[2]
    nk = kdim // TK_DOWN
    emit_h = g is not None
    row_blk = pl.BlockSpec((TM_DOWN, d), lambda i, k: (i, 0))
    in_specs = [pl.BlockSpec((TM_DOWN, TK_DOWN), lambda i, k: (i, k)),
                pl.BlockSpec((None, TK_DOWN, d), lambda i, k: (layer, k, 0)), row_blk]
    operands = [a, w, x]
    out_specs, out_shape = [row_blk], [jax.ShapeDtypeStruct((t, d), F32)]
    if emit_h:
        in_specs.append(pl.BlockSpec((1, d), lambda i, k: (0, 0)))
        operands.append(g.reshape(1, d))
        out_specs.append(row_blk)
        out_shape.append(jax.ShapeDtypeStruct((t, d), BF16))
    outs = pl.pallas_call(
        functools.partial(_mm_down_body, nk=nk, emit_h=emit_h),
        grid=(t // TM_DOWN, nk),
        in_specs=in_specs,
        out_specs=out_specs,
        out_shape=out_shape,
        compiler_params=_params(("arbitrary", "arbitrary")),
        name="mm_down",
    )(*operands)
    return outs if emit_h else outs[0]


def _mixer_tail(ya, yb, z_ref, p, mixw_ref, mixb_ref, y_ref, v_ref):
    rows = ya.shape[0]
    ya = jax.nn.silu(_ln(ya + p["cab"][...], p["lag"][...], p["lab"][...]))
    gg = p["gg"]
    y_ref[:, 0:D_A] = (_rms(ya) * gg[:, 0:D_A]).astype(y_ref.dtype)
    y_ref[:, D_A:D_A + D_B] = (_rms(yb) * gg[:, D_A:D_A + D_B]).astype(y_ref.dtype)

    u = jax.nn.gelu(z_ref[:, O_CU:O_CU + D_C])
    v = _ln(jax.nn.gelu(z_ref[:, O_CV:O_CV + D_C]), p["lcg"][...], p["lcb"][...])
    v_ref[...] = v
    vb = v.astype(BF16)
    for c in range(rows // CHUNK):
        r0 = c * CHUNK
        heads = []
        for h in range(H_C):
            l0 = h * HEAD_DIM
            mixed = jnp.dot(mixw_ref[h], vb[r0:r0 + CHUNK, l0:l0 + HEAD_DIM],
                            preferred_element_type=F32) + mixb_ref[h]
            heads.append(u[r0:r0 + CHUNK, l0:l0 + HEAD_DIM] * mixed)
        yc = jnp.concatenate(heads, axis=1)
        y_ref[r0:r0 + CHUNK, D_A + D_B:D_MODEL] = (
            _rms(yc) * gg[:, D_A + D_B:D_MODEL]).astype(y_ref.dtype)


def _mixer_prompt(z_ref, zh_ref, p, has_prev, y_ref, glu_ref, pre_ref, v_ref, exta, extb):
    tq = z_ref.shape[0]
    caw, cbw = p["caw"], p["cbw"]

    glu = z_ref[:, O_AVAL:O_AVAL + D_A] * jax.nn.sigmoid(z_ref[:, O_AGATE:O_AGATE + D_A])
    glu_ref[...] = glu
    exta[0:HALO, :] = jnp.where(has_prev, zh_ref[:, O_AVAL:O_AVAL + D_A]
                                * jax.nn.sigmoid(zh_ref[:, O_AGATE:O_AGATE + D_A]), 0.0)
    exta[HALO:HALO + tq, :] = glu
    off = HALO - (CONV_A_WIDTH - 1)
    ya = caw[0:1, :] * exta[off:off + tq, :]
    for k in range(1, CONV_A_WIDTH):
        ya = ya + caw[k:k + 1, :] * exta[off + k:off + k + tq, :]

    pre = z_ref[:, O_CGATE:O_CGATE + D_B] * z_ref[:, O_BIN:O_BIN + D_B]
    pre_ref[...] = pre
    extb[0:SUBLANES, :] = jnp.where(has_prev, zh_ref[HALO - SUBLANES:HALO, O_CGATE:O_CGATE + D_B]
                                    * zh_ref[HALO - SUBLANES:HALO, O_BIN:O_BIN + D_B], 0.0)
    extb[SUBLANES:SUBLANES + tq, :] = pre
    off = SUBLANES - (CONV_B_WIDTH - 1)
    yb = cbw[0:1, :] * extb[off:off + tq, :]
    for k in range(1, CONV_B_WIDTH):
        yb = yb + cbw[k:k + 1, :] * extb[off + k:off + k + tq, :]
    yb = z_ref[:, O_BGATE:O_BGATE + D_B] * yb

    _mixer_tail(ya, yb, z_ref, p, p["mixw_p"], p["mixb_p"], y_ref, v_ref)


def _shift_rows_in_seq(x, d, row_in_seq):
    if d == 0:
        return x
    return jnp.where(row_in_seq >= d, pltpu.roll(x, d, axis=0), 0.0)


def _mixer_sample(z_ref, ha_ref, hb_ref, p, y_ref, glu_ref, pre_ref, v_ref, hista):
    rows = z_ref.shape[0]
    nseq, seq = rows // SUBLANES, SUBLANES
    hist_len = CONV_A_WIDTH - 1
    caw, cbw = p["caw"], p["cbw"]

    glu = z_ref[:, O_AVAL:O_AVAL + D_A] * jax.nn.sigmoid(z_ref[:, O_AGATE:O_AGATE + D_A])
    glu_ref[...] = glu
    hista[...] = jnp.zeros_like(hista)
    hista[:, 0:hist_len, :] = ha_ref[...]
    row_a = lax.broadcasted_iota(jnp.int32, (rows, D_A), 0) % seq
    ya = caw[hist_len:hist_len + 1, :] * glu
    for d in range(1, seq):
        ya = ya + caw[hist_len - d:hist_len - d + 1, :] * _shift_rows_in_seq(glu, d, row_a)
    for k in range(hist_len):
        ya = ya + caw[k:k + 1, :] * hista[:, k:k + seq, :].reshape(rows, D_A)

    pre = z_ref[:, O_CGATE:O_CGATE + D_B] * z_ref[:, O_BIN:O_BIN + D_B]
    pre_ref[...] = pre
    row_b = lax.broadcasted_iota(jnp.int32, (rows, D_B), 0) % seq
    h0 = jnp.broadcast_to(hb_ref[:, 0:1, :], (nseq, seq, D_B)).reshape(rows, D_B)
    h1 = jnp.broadcast_to(hb_ref[:, 1:2, :], (nseq, seq, D_B)).reshape(rows, D_B)
    x1 = jnp.where(row_b == 0, h1, _shift_rows_in_seq(pre, 1, row_b))
    x0 = jnp.where(row_b == 0, h0, jnp.where(row_b == 1, h1, _shift_rows_in_seq(pre, 2, row_b)))
    yb = cbw[0:1, :] * x0 + cbw[1:2, :] * x1 + cbw[2:3, :] * pre
    yb = z_ref[:, O_BGATE:O_BGATE + D_B] * yb

    _mixer_tail(ya, yb, z_ref, p, p["mixw_s"], p["mixb_s"], y_ref, v_ref)


_MIXER_PARAMS = ("caw", "cab", "lag", "lab", "cbw", "lcg", "lcb", "gg",
                 "mixw_p", "mixb_p", "mixw_s", "mixb_s")


def _mixer_body(z_ref, zh_ref, ha_ref, hb_ref, *refs, n_prompt_blocks, blocks_per_seq):
    n_p = len(_MIXER_PARAMS)
    p = dict(zip(_MIXER_PARAMS, refs[:n_p]))
    y_ref, glu_ref, pre_ref, v_ref, exta, extb, hista = refs[n_p:]
    i = pl.program_id(0)

    @pl.when(i < n_prompt_blocks)
    def _():
        has_prev = lax.rem(i, blocks_per_seq) > 0
        _mixer_prompt(z_ref, zh_ref, p, has_prev, y_ref, glu_ref, pre_ref, v_ref, exta, extb)

    @pl.when(i >= n_prompt_blocks)
    def _():
        _mixer_sample(z_ref, ha_ref, hb_ref, p, y_ref, glu_ref, pre_ref, v_ref, hista)


def _mixers(z, hist_a, hist_b, layer, lw, n_prompt, seq_len):
    t = z.shape[0]
    tq = TM_ROWS
    n_prompt_blocks = n_prompt // tq
    nseq_blk = tq // SUBLANES
    halo_per_block = tq // HALO
    widths = (D_MODEL, D_A, D_B, D_C)
    dtypes = (BF16, F32, F32, F32)

    def sample_block(i):
        return jnp.maximum(i - n_prompt_blocks, 0)

    param_specs = [_full((CONV_A_WIDTH, D_A)), _full((1, D_A)), _full((1, D_A)), _full((1, D_A)),
                   _full((CONV_B_WIDTH, D_B)), _full((1, D_C)), _full((1, D_C)),
                   _full((1, D_MODEL))] + [_full((H_C, CHUNK, CHUNK))] * 4
    return pl.pallas_call(
        functools.partial(_mixer_body, n_prompt_blocks=n_prompt_blocks,
                          blocks_per_seq=seq_len // tq),
        grid=(t // tq,),
        in_specs=[pl.BlockSpec((tq, IN_COLS), lambda i: (i, 0)),
                  pl.BlockSpec((HALO, IN_COLS),
                               lambda i: (jnp.maximum(i * halo_per_block - 1, 0), 0)),
                  pl.BlockSpec((None, nseq_blk, CONV_A_WIDTH - 1, D_A),
                               lambda i: (layer, sample_block(i), 0, 0)),
                  pl.BlockSpec((None, nseq_blk, CONV_B_WIDTH - 1, D_B),
                               lambda i: (layer, sample_block(i), 0, 0))] + param_specs,
        out_specs=[pl.BlockSpec((tq, w), lambda i: (i, 0)) for w in widths],
        out_shape=[jax.ShapeDtypeStruct((t, w), dt) for w, dt in zip(widths, dtypes)],
        scratch_shapes=[pltpu.VMEM((HALO + tq, D_A), F32), pltpu.VMEM((SUBLANES + tq, D_B), F32),
                        pltpu.VMEM((nseq_blk, CONV_A_WIDTH - 1 + SUBLANES + 2, D_A), F32)],
        compiler_params=_params(("arbitrary",)),
        name="mixer",
    )(z, z, hist_a, hist_b, *[lw[name] for name in _MIXER_PARAMS])


def _router_body(x_ref, g_ref, rw_ref, meta_ref, cnt_ref, carry):
    @pl.when(pl.program_id(0) == 0)
    def _():
        carry[...] = jnp.zeros_like(carry)

    tm = x_ref.shape[0]
    h = _rms(x_ref[...]) * g_ref[...]
    logits = jnp.dot(h, rw_ref[...], precision=lax.Precision.HIGHEST, preferred_element_type=F32)
    lane = lax.broadcasted_iota(jnp.int32, (tm, LANES), 1).astype(F32)
    neg = -jnp.inf
    l1 = jnp.where(lane < N_EXPERTS, logits, neg)
    m1 = jnp.max(l1, axis=1, keepdims=True)
    i1 = jnp.min(jnp.where(l1 == m1, lane, float(LANES)), axis=1, keepdims=True)
    l2 = jnp.where(lane == i1, neg, l1)
    m2 = jnp.max(l2, axis=1, keepdims=True)
    i2 = jnp.min(jnp.where(l2 == m2, lane, float(LANES)), axis=1, keepdims=True)
    e = jnp.exp(m2 - m1)
    den = 1.0 + e
    g1 = 1.0 / den
    g2 = e / den

    oh1 = lane == i1
    oh2 = lane == i2
    onehot = jnp.where(oh1, 1.0, jnp.where(oh2, 1.0, 0.0))
    r = lax.broadcasted_iota(jnp.int32, (tm, tm), 0)
    c = lax.broadcasted_iota(jnp.int32, (tm, tm), 1)
    earlier = jnp.where(c < r, 1.0, 0.0).astype(BF16)
    before = jnp.dot(earlier, onehot.astype(BF16), preferred_element_type=F32) + carry[...]
    r1 = jnp.sum(jnp.where(oh1, before, 0.0), axis=1, keepdims=True)
    r2 = jnp.sum(jnp.where(oh2, before, 0.0), axis=1, keepdims=True)
    carry[...] += jnp.sum(onehot, axis=0, keepdims=True)
    cnt_ref[...] = carry[...]

    meta = jnp.where(lane == 0, i1, jnp.where(lane == 1, i2, jnp.where(lane == 2, r1, jnp.where(
        lane == 3, r2, jnp.where(lane == 4, g1, jnp.where(lane == 5, g2, 0.0))))))
    meta_ref[...] = meta


def _router(x, g, router_w):
    t, d = x.shape
    rw = jnp.pad(router_w, ((0, 0), (0, LANES - N_EXPERTS)))
    return pl.pallas_call(
        _router_body,
        grid=(t // TM_ROUTER,),
        in_specs=[pl.BlockSpec((TM_ROUTER, d), lambda i: (i, 0)), _full((1, d)),
                  _full((d, LANES))],
        out_specs=[pl.BlockSpec((TM_ROUTER, LANES), lambda i: (i, 0)), _full((1, LANES))],
        out_shape=[jax.ShapeDtypeStruct((t, LANES), F32), jax.ShapeDtypeStruct((1, LANES), F32)],
        scratch_shapes=[pltpu.VMEM((1, LANES), F32)],
        compiler_params=_params(("arbitrary",)),
        name="router",
    )(x, g.reshape(1, d), rw)


def _gather_rows(src_ref, copies, sem, n):
    def descriptors(r):
        return [pltpu.make_async_copy(src_ref.at[pl.ds(row, 1), :], dst.at[pl.ds(r, 1), :], sem)
                for row, dst in copies(r)]

    def issue(r, carry):
        for c in descriptors(r):
            c.start()
        return carry

    def wait(r, carry):
        for c in descriptors(r):
            c.wait()
        return carry

    lax.fori_loop(0, n, issue, 0, unroll=ROW_UNROLL)
    lax.fori_loop(0, n, wait, 0, unroll=ROW_UNROLL)


def _dispatch_body(tok, blk_valid, x_ref, g_ref, o_ref, buf, sem):
    n = pl.program_id(0)
    base = n * TM_ROWS

    @pl.when(blk_valid[n] == 1)
    def _():
        _gather_rows(x_ref, lambda r: [(tok[base + r], buf)], sem, TM_ROWS)
        o_ref[...] = (_rms(buf[...]) * g_ref[...]).astype(o_ref.dtype)

    @pl.when(blk_valid[n] == 0)
    def _():
        o_ref[...] = jnp.zeros_like(o_ref)


def _dispatch(tok_of_row, blk_valid, x, g):
    n_rows = tok_of_row.shape[0]
    d = x.shape[1]
    return pl.pallas_call(
        _dispatch_body,
        grid_spec=pltpu.PrefetchScalarGridSpec(
            num_scalar_prefetch=2,
            grid=(n_rows // TM_ROWS,),
            in_specs=[pl.BlockSpec(memory_space=pl.ANY),
                      pl.BlockSpec((1, d), lambda n, tok, v: (0, 0))],
            out_specs=pl.BlockSpec((TM_ROWS, d), lambda n, tok, v: (n, 0)),
            scratch_shapes=[pltpu.VMEM((TM_ROWS, d), F32), pltpu.SemaphoreType.DMA(())]),
        out_shape=jax.ShapeDtypeStruct((n_rows, d), BF16),
        compiler_params=_params(("arbitrary",)),
        name="dispatch",
    )(tok_of_row, blk_valid, x, g.reshape(1, d))


def _combine_body(pos, x_ref, meta_ref, g_ref, ys_ref, *refs, emit_x):
    out_refs, ybuf, sem = refs[:-2], refs[-2], refs[-1]
    base = pl.program_id(0) * TM_ROWS
    _gather_rows(ys_ref,
                 lambda r: [(pos[TOP_K * (base + r) + k], ybuf.at[k]) for k in range(TOP_K)],
                 sem, TM_ROWS)

    xn = x_ref[...] + (meta_ref[:, 4:5] * ybuf[0] + meta_ref[:, 5:6] * ybuf[1])
    h = _rms(xn) * g_ref[...]
    if emit_x:
        out_refs[0][...] = xn
        out_refs[1][...] = h.astype(out_refs[1].dtype)
    else:
        out_refs[0][...] = h.astype(out_refs[0].dtype)


def _combine(pos_flat, x, meta, g, ys, emit_x):
    t, d = x.shape
    blk = pl.BlockSpec((TM_ROWS, d), lambda i, p: (i, 0))
    if emit_x:
        out_specs = [blk, blk]
        out_shape = [jax.ShapeDtypeStruct((t, d), F32), jax.ShapeDtypeStruct((t, d), BF16)]
    else:
        out_specs = [blk]
        out_shape = [jax.ShapeDtypeStruct((t, d), F32)]
    return pl.pallas_call(
        functools.partial(_combine_body, emit_x=emit_x),
        grid_spec=pltpu.PrefetchScalarGridSpec(
            num_scalar_prefetch=1,
            grid=(t // TM_ROWS,),
            in_specs=[blk, pl.BlockSpec((TM_ROWS, LANES), lambda i, p: (i, 0)),
                      pl.BlockSpec((1, d), lambda i, p: (0, 0)),
                      pl.BlockSpec(memory_space=pl.ANY)],
            out_specs=out_specs,
            scratch_shapes=[pltpu.VMEM((TOP_K, TM_ROWS, d), F32), pltpu.SemaphoreType.DMA(())]),
        out_shape=out_shape,
        compiler_params=_params(("arbitrary",)),
        name="combine",
    )(pos_flat, x, meta, g.reshape(1, d), ys)


def _gmm_up_body(it_e, it_jw, it_xblk, it_oblk, it_jo, it_first, it_rows,
                 x_ref, w1_ref, w3_ref, o_ref, wbf):
    del it_e, it_jw, it_xblk, it_oblk, it_jo
    n = pl.program_id(0)
    rows = it_rows[n]

    @pl.when(it_first[n] == 1)
    def _():
        wbf[0] = w1_ref[...].astype(BF16)
        wbf[1] = w3_ref[...].astype(BF16)

    for q in range(TM_MOE // TM_SUB):
        sub = pl.ds(q * TM_SUB, TM_SUB)

        @pl.when(rows > q * TM_SUB)
        def _():
            _up_compute(x_ref.at[sub, :], wbf, o_ref.at[sub, :], 2)

        @pl.when(rows <= q * TM_SUB)
        def _():
            o_ref[sub, :] = jnp.zeros((TM_SUB, o_ref.shape[1]), o_ref.dtype)


def _gmm_up(items, xs, w1, w3, layer):
    n_rows, k = xs.shape
    n_ff = w1.shape[3]
    n_items = items[0].shape[0]

    def w_map(n, e, jw, xb, ob, jo, f, v):
        return (layer, e[n], 0, jw[n])

    return pl.pallas_call(
        _gmm_up_body,
        grid_spec=pltpu.PrefetchScalarGridSpec(
            num_scalar_prefetch=7,
            grid=(n_items,),
            in_specs=[pl.BlockSpec((TM_MOE, k), lambda n, e, jw, xb, ob, jo, f, v: (xb[n], 0)),
                      pl.BlockSpec((None, None, k, TN_UP2), w_map),
                      pl.BlockSpec((None, None, k, TN_UP2), w_map)],
            out_specs=pl.BlockSpec((TM_MOE, TN_UP2),
                                   lambda n, e, jw, xb, ob, jo, f, v: (ob[n], jo[n])),
            scratch_shapes=[pltpu.VMEM((2, k, TN_UP2), BF16)]),
        out_shape=jax.ShapeDtypeStruct((n_rows, n_ff), BF16),
        compiler_params=_params(("arbitrary",)),
        name="gmm_up",
    )(*items, xs, w1, w3)


def _gmm_down_body(t_e, t_ablk, t_rows, a_ref, w_ref, o_ref, wbf):
    del t_e, t_ablk
    n = pl.program_id(0)
    k = pl.program_id(1)
    rows = t_rows[n]

    @pl.when(k == 0)
    def _():
        o_ref[...] = jnp.zeros_like(o_ref)

    @pl.when(rows > 0)
    def _():
        wbf[...] = w_ref[...].astype(BF16)

    for q in range(TM_MOE // TM_SUB):
        sub = pl.ds(q * TM_SUB, TM_SUB)

        @pl.when(rows > q * TM_SUB)
        def _():
            o_ref[sub, :] = jnp.dot(a_ref[sub, :], wbf[...],
                                    preferred_element_type=F32) + o_ref[sub, :]


def _gmm_down(tiles, a, w2, layer):
    n_rows, n_ff = a.shape
    d = w2.shape[3]
    tk = TK_MOE
    nk = n_ff // tk
    n_tiles = tiles[0].shape[0]

    def kk(n, k, v):
        return jnp.where(v[n] > 0, k, nk - 1)

    return pl.pallas_call(
        _gmm_down_body,
        grid_spec=pltpu.PrefetchScalarGridSpec(
            num_scalar_prefetch=3,
            grid=(n_tiles, nk),
            in_specs=[pl.BlockSpec((TM_MOE, tk), lambda n, k, e, b, v: (b[n], kk(n, k, v))),
                      pl.BlockSpec((None, None, tk, d),
                                   lambda n, k, e, b, v: (layer, e[n], kk(n, k, v), 0))],
            out_specs=pl.BlockSpec((TM_MOE, d), lambda n, k, e, b, v: (n, 0)),
            scratch_shapes=[pltpu.VMEM((tk, d), BF16)]),
        out_shape=jax.ShapeDtypeStruct((n_rows, d), F32),
        compiler_params=_params(("arbitrary", "arbitrary")),
        name="gmm_down",
    )(*tiles, a, w2)


def _moe_plan(meta, counts, n_tokens, n_col_blocks):
    nt_max = (TOP_K * n_tokens) // TM_MOE + N_EXPERTS
    n_rows = nt_max * TM_MOE
    cnt = counts[0, :N_EXPERTS].astype(jnp.int32)
    tiles = (cnt + TM_MOE - 1) // TM_MOE
    tiles_end = jnp.cumsum(tiles)
    tiles_start = tiles_end - tiles
    n_tiles = tiles_end[-1]

    ids = meta[:, 0:2].astype(jnp.int32)
    rank = meta[:, 2:4].astype(jnp.int32)
    pos = (tiles_start[ids] * TM_MOE + rank).reshape(-1)
    token = jnp.repeat(jnp.arange(n_tokens, dtype=jnp.int32), TOP_K)
    tok_of_row = jnp.zeros((n_rows,), jnp.int32).at[pos].set(token, unique_indices=True)
    blk_valid = (jnp.arange(n_rows // TM_ROWS, dtype=jnp.int32)
                 < n_tiles * (TM_MOE // TM_ROWS)).astype(jnp.int32)

    def owner(idx, ends):
        return jnp.sum((ends[None, :] <= idx[:, None]).astype(jnp.int32), axis=1)

    def rows_in_tile(e, local_tile, valid):
        return jnp.where(valid, jnp.minimum(cnt[e] - local_tile * TM_MOE, TM_MOE), 0).astype(jnp.int32)

    tile_idx = jnp.arange(nt_max, dtype=jnp.int32)
    t_valid = tile_idx < n_tiles
    t_idx = jnp.minimum(tile_idx, n_tiles - 1)
    t_e = owner(t_idx, tiles_end)
    tiles_plan = (t_e, t_idx, rows_in_tile(t_e, t_idx - tiles_start[t_e], t_valid))

    item_idx = jnp.arange(nt_max * n_col_blocks, dtype=jnp.int32)
    n_items = n_tiles * n_col_blocks
    i_valid = item_idx < n_items
    i_idx = jnp.minimum(item_idx, n_items - 1)
    i_e = owner(i_idx, tiles_end * n_col_blocks)
    local = i_idx - tiles_start[i_e] * n_col_blocks
    te = jnp.maximum(tiles[i_e], 1)
    i_j = (local // te).astype(jnp.int32)
    i_l = local % te
    i_blk = (tiles_start[i_e] + i_l).astype(jnp.int32)
    i_first = jnp.logical_and(i_l == 0, i_valid)
    tail = jnp.maximum(item_idx - n_items, 0)
    o_blk = jnp.where(i_valid, i_blk, n_tiles + tail // n_col_blocks).astype(jnp.int32)
    o_j = jnp.where(i_valid, i_j, tail % n_col_blocks).astype(jnp.int32)
    items_plan = (i_e, i_j, i_blk, o_blk, o_j, i_first.astype(jnp.int32),
                  rows_in_tile(i_e, i_l, i_valid))
    return pos, tok_of_row, blk_valid, tiles_plan, items_plan


def _moe(x, g, router_w, w1, w3, w2, layer, g_next, emit_x):
    t = x.shape[0]
    meta, counts = _router(x, g, router_w)
    pos, tok_of_row, blk_valid, tiles_plan, items_plan = _moe_plan(
        meta, counts, t, w1.shape[3] // TN_UP2)
    xs = _dispatch(tok_of_row, blk_valid, x, g)
    a = _gmm_up(items_plan, xs, w1, w3, layer)
    ys = _gmm_down(tiles_plan, a, w2, layer)
    return _combine(pos, x, meta, g_next, ys, emit_x)


def _mix_params(spatial_w, spatial_b, sample_len):
    mask = jnp.tril(jnp.ones((CHUNK, CHUNK), dtype=bool))
    w = jnp.where(mask[None], spatial_w, 0)
    b_prompt = jnp.broadcast_to(spatial_b[:, :, None], (H_C, CHUNK, HEAD_DIM))
    eye = jnp.eye(CHUNK // sample_len, dtype=w.dtype)
    w_small = w[:, :sample_len, :sample_len]
    w_sample = jnp.einsum("ab,hts->hatbs", eye, w_small).reshape(H_C, CHUNK, CHUNK)
    b_small = jnp.tile(spatial_b[:, :sample_len], (1, CHUNK // sample_len))
    b_sample = jnp.broadcast_to(b_small[:, :, None], (H_C, CHUNK, HEAD_DIM))
    return w.astype(BF16), b_prompt, w_sample.astype(BF16), b_sample


def kernel(x_prompt, x_sample, state_conv_a, state_conv_b, norm1_g, norm2_g, final_g, w_in, w_out,
           conv_a_w, conv_a_b, ln_a_g, ln_a_b, conv_b_w, ln_c_g, ln_c_b, spatial_w, spatial_b,
           group_g, ffn_w1, ffn_w3, ffn_w2, router_w, moe_w1, moe_w3, moe_w2):
    n_batch, seq_len, d = x_prompt.shape
    n_dec, dec_len, _ = x_sample.shape
    n_prompt = n_batch * seq_len
    assert dec_len == SUBLANES and d == D_MODEL

    x = jnp.concatenate([x_prompt.reshape(n_prompt, d), x_sample.reshape(n_dec * dec_len, d)], axis=0)
    h = _norm(x, norm1_g[0])

    glus, pres, vs = [], [], []
    for l in range(DEPTH):
        mw_p, mb_p, mw_s, mb_s = _mix_params(spatial_w[l], spatial_b[l], dec_len)
        lw = dict(caw=conv_a_w[l], cab=conv_a_b[l].reshape(1, D_A),
                  lag=ln_a_g[l].reshape(1, D_A), lab=ln_a_b[l].reshape(1, D_A),
                  cbw=conv_b_w[l], lcg=ln_c_g[l].reshape(1, D_C),
                  lcb=ln_c_b[l].reshape(1, D_C), gg=group_g[l].reshape(1, D_MODEL),
                  mixw_p=mw_p, mixb_p=mb_p, mixw_s=mw_s, mixb_s=mb_s)
        z = _mm_up(h, [w_in], l, F32)
        y, glu, pre, v = _mixers(z, state_conv_a, state_conv_b, l, lw, n_prompt, seq_len)
        glus.append(glu)
        pres.append(pre)
        vs.append(v)
        last = l == DEPTH - 1
        g_next = final_g if last else norm1_g[l + 1]
        i = l // 2
        if l % 2 == 0:
            x, h = _mm_down(y, w_out, l, x, norm2_g[l])
            a = _mm_up(h, [ffn_w1, ffn_w3], i, BF16)
            x, h = _mm_down(a, ffn_w2, i, x, g_next)
        else:
            x = _mm_down(y, w_out, l, x)
            outs = _moe(x, norm2_g[l], router_w[i], moe_w1, moe_w3, moe_w2, i, g_next, not last)
            if last:
                y_final = outs[0]
            else:
                x, h = outs

    def prompt_rows(a):
        return a[:n_prompt].reshape(n_batch, seq_len, a.shape[-1])

    def sample_rows(a):
        return a[n_prompt:].reshape(n_dec, dec_len, a.shape[-1])

    ka, kb = CONV_A_WIDTH - 1, CONV_B_WIDTH - 1
    chunk_start = ((seq_len - 1) // CHUNK) * CHUNK
    pa = jnp.stack([prompt_rows(g)[:, seq_len - ka:] for g in glus])
    pb = jnp.stack([prompt_rows(p)[:, seq_len - kb:] for p in pres])
    pv = jnp.stack([prompt_rows(v)[:, chunk_start:] for v in vs])
    sa = jnp.stack([jnp.concatenate([state_conv_a[l][:, dec_len:], sample_rows(glus[l])], axis=1)
                    for l in range(DEPTH)])
    sb = jnp.stack([sample_rows(p)[:, dec_len - kb:] for p in pres])
    sv = jnp.stack([sample_rows(v) for v in vs])
    return (prompt_rows(y_final), sample_rows(y_final), pa, pb, pv, sa, sb, sv)
```

```python
import functools

import jax
import jax.numpy as jnp
from jax import lax
from jax.experimental import pallas as pl
from jax.experimental.pallas import tpu as pltpu

F32 = jnp.float32
BF16 = jnp.bfloat16

D_MODEL = 2048
DEPTH = 4
HEAD_DIM = 128
D_A = 6 * HEAD_DIM
D_B = 5 * HEAD_DIM
D_C = 5 * HEAD_DIM
H_C = 5
CONV_A_WIDTH = 31
CONV_B_WIDTH = 3
CHUNK = 128
IN_COLS = 2 * D_A + 3 * D_B + 2 * D_C
N_EXPERTS = 8
TOP_K = 2
RMS_EPS = 1e-6
LN_EPS = 1e-5

O_AVAL = 0
O_AGATE = D_A
O_BGATE = 2 * D_A
O_CGATE = 2 * D_A + D_B
O_BIN = 2 * D_A + 2 * D_B
O_CU = 2 * D_A + 3 * D_B
O_CV = 2 * D_A + 3 * D_B + D_C

LANES = 128
SUBLANES = 8
VMEM_LIMIT_BYTES = 56 * 1024 * 1024

TM_UP = 1024
TN_UP1 = 1024
TN_UP2 = 512
TM_DOWN = 768
TK_DOWN = 512
TK_MOE = 1024
TM_MOE = 1024
TM_SUB = 256
TM_ROUTER = 512
ROW_UNROLL = 8
TM_ROWS = 256
TM_DISPATCH = 512
HALO = 32


def _params(sem):
    return pltpu.CompilerParams(dimension_semantics=sem, vmem_limit_bytes=VMEM_LIMIT_BYTES)


def _rms(x):
    return x * lax.rsqrt(jnp.mean(x * x, axis=-1, keepdims=True) + RMS_EPS)


def _ln(x, g, b):
    mu = jnp.mean(x, axis=-1, keepdims=True)
    xc = x - mu
    var = jnp.mean(xc * xc, axis=-1, keepdims=True)
    return xc * lax.rsqrt(var + LN_EPS) * g + b


def _full(shape):
    nd = len(shape)
    return pl.BlockSpec(shape, lambda *_: (0,) * nd)


def _norm_body(x_ref, g_ref, o_ref):
    o_ref[...] = (_rms(x_ref[...]) * g_ref[...]).astype(o_ref.dtype)


def _norm(x, g):
    t, d = x.shape
    return pl.pallas_call(
        _norm_body,
        grid=(t // TM_DOWN,),
        in_specs=[pl.BlockSpec((TM_DOWN, d), lambda i: (i, 0)),
                  pl.BlockSpec((1, d), lambda i: (0, 0))],
        out_specs=pl.BlockSpec((TM_DOWN, d), lambda i: (i, 0)),
        out_shape=jax.ShapeDtypeStruct((t, d), BF16),
        compiler_params=_params(("arbitrary",)),
        name="norm_in",
    )(x, g.reshape(1, d))


def _up_compute(x_ref, wbf, o_ref, n_w):
    x = x_ref[...]
    a = jnp.dot(x, wbf[0], preferred_element_type=F32)
    if n_w == 2:
        b = jnp.dot(x, wbf[1], preferred_element_type=F32)
        a = jax.nn.silu(a) * b
    o_ref[...] = a.astype(o_ref.dtype)


def _mm_up_body(x_ref, *refs, n_w):
    w_refs, o_ref, wbf = refs[:n_w], refs[n_w], refs[n_w + 1]

    @pl.when(pl.program_id(1) == 0)
    def _():
        for n in range(n_w):
            wbf[n] = w_refs[n][...].astype(BF16)

    _up_compute(x_ref, wbf, o_ref, n_w)


def _mm_up(x, ws, layer, out_dtype):
    t, k = x.shape
    n = ws[0].shape[2]
    n_w = len(ws)
    tn = TN_UP1 if n_w == 1 else TN_UP2
    return pl.pallas_call(
        functools.partial(_mm_up_body, n_w=n_w),
        grid=(pl.cdiv(n, tn), t // TM_UP),
        in_specs=[pl.BlockSpec((TM_UP, k), lambda j, i: (i, 0))]
        + [pl.BlockSpec((None, k, tn), lambda j, i: (layer, 0, j)) for _ in ws],
        out_specs=pl.BlockSpec((TM_UP, tn), lambda j, i: (i, j)),
        out_shape=jax.ShapeDtypeStruct((t, n), out_dtype),
        scratch_shapes=[pltpu.VMEM((n_w, k, tn), BF16)],
        compiler_params=_params(("arbitrary", "arbitrary")),
        name="mm_up%d" % n_w,
    )(x, *ws)


def _mm_down_body(a_ref, w_ref, x_ref, *refs, nk, emit_h):
    xo_ref = refs[-2] if emit_h else refs[-1]
    k = pl.program_id(1)

    def partial_product():
        return jnp.dot(a_ref[...], w_ref[...].astype(BF16), preferred_element_type=F32)

    @pl.when(k == 0)
    def _():
        xo_ref[...] = partial_product() + x_ref[...]

    @pl.when(k > 0)
    def _():
        xo_ref[...] = partial_product() + xo_ref[...]

    if emit_h:
        g_ref, ho_ref = refs[0], refs[-1]

        @pl.when(k == nk - 1)
        def _():
            ho_ref[...] = (_rms(xo_ref[...]) * g_ref[...]).astype(ho_ref.dtype)


def _mm_down(a, w, layer, x, g=None):
    t, kdim = a.shape
    d = w.shape[2]
    nk = kdim // TK_DOWN
    emit_h = g is not None
    row_blk = pl.BlockSpec((TM_DOWN, d), lambda i, k: (i, 0))
    in_specs = [pl.BlockSpec((TM_DOWN, TK_DOWN), lambda i, k: (i, k)),
                pl.BlockSpec((None, TK_DOWN, d), lambda i, k: (layer, k, 0)), row_blk]
    operands = [a, w, x]
    out_specs, out_shape = [row_blk], [jax.ShapeDtypeStruct((t, d), F32)]
    if emit_h:
        in_specs.append(pl.BlockSpec((1, d), lambda i, k: (0, 0)))
        operands.append(g.reshape(1, d))
        out_specs.append(row_blk)
        out_shape.append(jax.ShapeDtypeStruct((t, d), BF16))
    outs = pl.pallas_call(
        functools.partial(_mm_down_body, nk=nk, emit_h=emit_h),
        grid=(t // TM_DOWN, nk),
        in_specs=in_specs,
        out_specs=out_specs,
        out_shape=out_shape,
        compiler_params=_params(("arbitrary", "arbitrary")),
        name="mm_down",
    )(*operands)
    return outs if emit_h else outs[0]


def _mixer_tail(ya, yb, z_ref, p, mixw_ref, mixb_ref, y_ref, v_ref):
    rows = ya.shape[0]
    ya = jax.nn.silu(_ln(ya + p["cab"][...], p["lag"][...], p["lab"][...]))
    gg = p["gg"]
    y_ref[:, 0:D_A] = (_rms(ya) * gg[:, 0:D_A]).astype(y_ref.dtype)
    y_ref[:, D_A:D_A + D_B] = (_rms(yb) * gg[:, D_A:D_A + D_B]).astype(y_ref.dtype)

    u = jax.nn.gelu(z_ref[:, O_CU:O_CU + D_C])
    v = _ln(jax.nn.gelu(z_ref[:, O_CV:O_CV + D_C]), p["lcg"][...], p["lcb"][...])
    v_ref[...] = v
    vb = v.astype(BF16)
    for c in range(rows // CHUNK):
        r0 = c * CHUNK
        heads = []
        for h in range(H_C):
            l0 = h * HEAD_DIM
            mixed = jnp.dot(mixw_ref[h], vb[r0:r0 + CHUNK, l0:l0 + HEAD_DIM],
                            preferred_element_type=F32) + mixb_ref[h]
            heads.append(u[r0:r0 + CHUNK, l0:l0 + HEAD_DIM] * mixed)
        yc = jnp.concatenate(heads, axis=1)
        y_ref[r0:r0 + CHUNK, D_A + D_B:D_MODEL] = (
            _rms(yc) * gg[:, D_A + D_B:D_MODEL]).astype(y_ref.dtype)


def _mixer_prompt(z_ref, zh_ref, p, has_prev, y_ref, glu_ref, pre_ref, v_ref, exta, extb):
    tq = z_ref.shape[0]
    caw, cbw = p["caw"], p["cbw"]

    glu = z_ref[:, O_AVAL:O_AVAL + D_A] * jax.nn.sigmoid(z_ref[:, O_AGATE:O_AGATE + D_A])
    glu_ref[...] = glu
    exta[0:HALO, :] = jnp.where(has_prev, zh_ref[:, O_AVAL:O_AVAL + D_A]
                                * jax.nn.sigmoid(zh_ref[:, O_AGATE:O_AGATE + D_A]), 0.0)
    exta[HALO:HALO + tq, :] = glu
    off = HALO - (CONV_A_WIDTH - 1)
    ya = caw[0:1, :] * exta[off:off + tq, :]
    for k in range(1, CONV_A_WIDTH):
        ya = ya + caw[k:k + 1, :] * exta[off + k:off + k + tq, :]

    pre = z_ref[:, O_CGATE:O_CGATE + D_B] * z_ref[:, O_BIN:O_BIN + D_B]
    pre_ref[...] = pre
    extb[0:SUBLANES, :] = jnp.where(has_prev, zh_ref[HALO - SUBLANES:HALO, O_CGATE:O_CGATE + D_B]
                                    * zh_ref[HALO - SUBLANES:HALO, O_BIN:O_BIN + D_B], 0.0)
    extb[SUBLANES:SUBLANES + tq, :] = pre
    off = SUBLANES - (CONV_B_WIDTH - 1)
    yb = cbw[0:1, :] * extb[off:off + tq, :]
    for k in range(1, CONV_B_WIDTH):
        yb = yb + cbw[k:k + 1, :] * extb[off + k:off + k + tq, :]
    yb = z_ref[:, O_BGATE:O_BGATE + D_B] * yb

    _mixer_tail(ya, yb, z_ref, p, p["mixw_p"], p["mixb_p"], y_ref, v_ref)


def _shift_rows_in_seq(x, d, row_in_seq):
    if d == 0:
        return x
    return jnp.where(row_in_seq >= d, pltpu.roll(x, d, axis=0), 0.0)


def _mixer_sample(z_ref, ha_ref, hb_ref, p, y_ref, glu_ref, pre_ref, v_ref, hista):
    rows = z_ref.shape[0]
    nseq, seq = rows // SUBLANES, SUBLANES
    hist_len = CONV_A_WIDTH - 1
    caw, cbw = p["caw"], p["cbw"]

    glu = z_ref[:, O_AVAL:O_AVAL + D_A] * jax.nn.sigmoid(z_ref[:, O_AGATE:O_AGATE + D_A])
    glu_ref[...] = glu
    hista[...] = jnp.zeros_like(hista)
    hista[:, 0:hist_len, :] = ha_ref[...]
    row_a = lax.broadcasted_iota(jnp.int32, (rows, D_A), 0) % seq
    ya = caw[hist_len:hist_len + 1, :] * glu
    for d in range(1, seq):
        ya = ya + caw[hist_len - d:hist_len - d + 1, :] * _shift_rows_in_seq(glu, d, row_a)
    for k in range(hist_len):
        ya = ya + caw[k:k + 1, :] * hista[:, k:k + seq, :].reshape(rows, D_A)

    pre = z_ref[:, O_CGATE:O_CGATE + D_B] * z_ref[:, O_BIN:O_BIN + D_B]
    pre_ref[...] = pre
    row_b = lax.broadcasted_iota(jnp.int32, (rows, D_B), 0) % seq
    h0 = jnp.broadcast_to(hb_ref[:, 0:1, :], (nseq, seq, D_B)).reshape(rows, D_B)
    h1 = jnp.broadcast_to(hb_ref[:, 1:2, :], (nseq, seq, D_B)).reshape(rows, D_B)
    x1 = jnp.where(row_b == 0, h1, _shift_rows_in_seq(pre, 1, row_b))
    x0 = jnp.where(row_b == 0, h0, jnp.where(row_b == 1, h1, _shift_rows_in_seq(pre, 2, row_b)))
    yb = cbw[0:1, :] * x0 + cbw[1:2, :] * x1 + cbw[2:3, :] * pre
    yb = z_ref[:, O_BGATE:O_BGATE + D_B] * yb

    _mixer_tail(ya, yb, z_ref, p, p["mixw_s"], p["mixb_s"], y_ref, v_ref)


_MIXER_PARAMS = ("caw", "cab", "lag", "lab", "cbw", "lcg", "lcb", "gg",
                 "mixw_p", "mixb_p", "mixw_s", "mixb_s")


def _mixer_body(z_ref, zh_ref, ha_ref, hb_ref, *refs, n_prompt_blocks, blocks_per_seq):
    n_p = len(_MIXER_PARAMS)
    p = dict(zip(_MIXER_PARAMS, refs[:n_p]))
    y_ref, glu_ref, pre_ref, v_ref, exta, extb, hista = refs[n_p:]
    i = pl.program_id(0)

    @pl.when(i < n_prompt_blocks)
    def _():
        has_prev = lax.rem(i, blocks_per_seq) > 0
        _mixer_prompt(z_ref, zh_ref, p, has_prev, y_ref, glu_ref, pre_ref, v_ref, exta, extb)

    @pl.when(i >= n_prompt_blocks)
    def _():
        _mixer_sample(z_ref, ha_ref, hb_ref, p, y_ref, glu_ref, pre_ref, v_ref, hista)


def _mixers(z, hist_a, hist_b, layer, lw, n_prompt, seq_len):
    t = z.shape[0]
    tq = TM_ROWS
    n_prompt_blocks = n_prompt // tq
    nseq_blk = tq // SUBLANES
    halo_per_block = tq // HALO
    widths = (D_MODEL, D_A, D_B, D_C)
    dtypes = (BF16, F32, F32, F32)

    def sample_block(i):
        return jnp.maximum(i - n_prompt_blocks, 0)

    param_specs = [_full((CONV_A_WIDTH, D_A)), _full((1, D_A)), _full((1, D_A)), _full((1, D_A)),
                   _full((CONV_B_WIDTH, D_B)), _full((1, D_C)), _full((1, D_C)),
                   _full((1, D_MODEL))] + [_full((H_C, CHUNK, CHUNK))] * 4
    return pl.pallas_call(
        functools.partial(_mixer_body, n_prompt_blocks=n_prompt_blocks,
                          blocks_per_seq=seq_len // tq),
        grid=(t // tq,),
        in_specs=[pl.BlockSpec((tq, IN_COLS), lambda i: (i, 0)),
                  pl.BlockSpec((HALO, IN_COLS),
                               lambda i: (jnp.maximum(i * halo_per_block - 1, 0), 0)),
                  pl.BlockSpec((None, nseq_blk, CONV_A_WIDTH - 1, D_A),
                               lambda i: (layer, sample_block(i), 0, 0)),
                  pl.BlockSpec((None, nseq_blk, CONV_B_WIDTH - 1, D_B),
                               lambda i: (layer, sample_block(i), 0, 0))] + param_specs,
        out_specs=[pl.BlockSpec((tq, w), lambda i: (i, 0)) for w in widths],
        out_shape=[jax.ShapeDtypeStruct((t, w), dt) for w, dt in zip(widths, dtypes)],
        scratch_shapes=[pltpu.VMEM((HALO + tq, D_A), F32), pltpu.VMEM((SUBLANES + tq, D_B), F32),
                        pltpu.VMEM((nseq_blk, CONV_A_WIDTH - 1 + SUBLANES + 2, D_A), F32)],
        compiler_params=_params(("arbitrary",)),
        name="mixer",
    )(z, z, hist_a, hist_b, *[lw[name] for name in _MIXER_PARAMS])


def _router_body(x_ref, g_ref, rw_ref, meta_ref, cnt_ref, carry):
    @pl.when(pl.program_id(0) == 0)
    def _():
        carry[...] = jnp.zeros_like(carry)

    tm = x_ref.shape[0]
    h = _rms(x_ref[...]) * g_ref[...]
    logits = jnp.dot(h, rw_ref[...], preferred_element_type=F32)
    lane = lax.broadcasted_iota(jnp.int32, (tm, LANES), 1).astype(F32)
    neg = -jnp.inf
    l1 = jnp.where(lane < N_EXPERTS, logits, neg)
    m1 = jnp.max(l1, axis=1, keepdims=True)
    i1 = jnp.min(jnp.where(l1 == m1, lane, float(LANES)), axis=1, keepdims=True)
    l2 = jnp.where(lane == i1, neg, l1)
    m2 = jnp.max(l2, axis=1, keepdims=True)
    i2 = jnp.min(jnp.where(l2 == m2, lane, float(LANES)), axis=1, keepdims=True)
    e = jnp.exp(m2 - m1)
    den = 1.0 + e
    g1 = 1.0 / den
    g2 = e / den

    oh1 = lane == i1
    oh2 = lane == i2
    onehot = jnp.where(oh1, 1.0, jnp.where(oh2, 1.0, 0.0))
    r = lax.broadcasted_iota(jnp.int32, (tm, tm), 0)
    c = lax.broadcasted_iota(jnp.int32, (tm, tm), 1)
    earlier = jnp.where(c < r, 1.0, 0.0).astype(BF16)
    before = jnp.dot(earlier, onehot.astype(BF16), preferred_element_type=F32) + carry[...]
    r1 = jnp.sum(jnp.where(oh1, before, 0.0), axis=1, keepdims=True)
    r2 = jnp.sum(jnp.where(oh2, before, 0.0), axis=1, keepdims=True)
    carry[...] += jnp.sum(onehot, axis=0, keepdims=True)
    cnt_ref[...] = carry[...]

    meta = jnp.where(lane == 0, i1, jnp.where(lane == 1, i2, jnp.where(lane == 2, r1, jnp.where(
        lane == 3, r2, jnp.where(lane == 4, g1, jnp.where(lane == 5, g2, 0.0))))))
    meta_ref[...] = meta


def _router(x, g, router_w):
    t, d = x.shape
    rw = jnp.pad(router_w, ((0, 0), (0, LANES - N_EXPERTS)))
    return pl.pallas_call(
        _router_body,
        grid=(t // TM_ROUTER,),
        in_specs=[pl.BlockSpec((TM_ROUTER, d), lambda i: (i, 0)), _full((1, d)),
                  _full((d, LANES))],
        out_specs=[pl.BlockSpec((TM_ROUTER, LANES), lambda i: (i, 0)), _full((1, LANES))],
        out_shape=[jax.ShapeDtypeStruct((t, LANES), F32), jax.ShapeDtypeStruct((1, LANES), F32)],
        scratch_shapes=[pltpu.VMEM((1, LANES), F32)],
        compiler_params=_params(("arbitrary",)),
        name="router",
    )(x, g.reshape(1, d), rw)


class _RowGather:
    def __init__(self, src_ref, copies, sem, n):
        self.src_ref, self.copies, self.sem, self.n = src_ref, copies, sem, n

    def _descriptors(self, r):
        return [pltpu.make_async_copy(self.src_ref.at[pl.ds(row, 1), :],
                                      dst.at[pl.ds(r, 1), :], self.sem)
                for row, dst in self.copies(r)]

    def start(self):
        def body(r, carry):
            for c in self._descriptors(r):
                c.start()
            return carry
        lax.fori_loop(0, self.n, body, 0, unroll=ROW_UNROLL)

    def wait(self):
        def body(r, carry):
            for c in self._descriptors(r):
                c.wait()
            return carry
        lax.fori_loop(0, self.n, body, 0, unroll=ROW_UNROLL)


def _prefetched_gather(step, n_active, gather_of):
    slot = lax.rem(step, 2)

    @pl.when(jnp.logical_and(step == 0, n_active > 0))
    def _():
        gather_of(step, slot).start()

    @pl.when(step + 1 < n_active)
    def _():
        gather_of(step + 1, 1 - slot).start()

    @pl.when(step < n_active)
    def _():
        gather_of(step, slot).wait()

    return slot


def _dispatch_body(tok, blk_order, n_active_ref, x_ref, g_ref, o_ref, buf, sem):
    n = pl.program_id(0)
    n_active = n_active_ref[0]

    def gather_of(step, slot):
        base = blk_order[step] * TM_DISPATCH
        return _RowGather(x_ref, lambda r: [(tok[base + r], buf.at[slot])], sem.at[slot],
                          TM_DISPATCH)

    slot = _prefetched_gather(n, n_active, gather_of)

    @pl.when(n < n_active)
    def _():
        o_ref[...] = (_rms(buf[slot]) * g_ref[...]).astype(o_ref.dtype)

    @pl.when(n >= n_active)
    def _():
        o_ref[...] = jnp.zeros_like(o_ref)


def _dispatch(tok_of_row, blk_order, n_active, x, g):
    n_rows = tok_of_row.shape[0]
    d = x.shape[1]
    return pl.pallas_call(
        _dispatch_body,
        grid_spec=pltpu.PrefetchScalarGridSpec(
            num_scalar_prefetch=3,
            grid=(n_rows // TM_DISPATCH,),
            in_specs=[pl.BlockSpec(memory_space=pl.ANY),
                      pl.BlockSpec((1, d), lambda n, tok, order, na: (0, 0))],
            out_specs=pl.BlockSpec((TM_DISPATCH, d), lambda n, tok, order, na: (order[n], 0)),
            scratch_shapes=[pltpu.VMEM((2, TM_DISPATCH, d), F32), pltpu.SemaphoreType.DMA((2,))]),
        out_shape=jax.ShapeDtypeStruct((n_rows, d), BF16),
        compiler_params=_params(("arbitrary",)),
        name="dispatch",
    )(tok_of_row, blk_order, n_active, x, g.reshape(1, d))


def _combine_body(pos, x_ref, meta_ref, g_ref, ys_ref, *refs, emit_x):
    out_refs, ybuf, sem = refs[:-2], refs[-2], refs[-1]
    n = pl.program_id(0)

    def gather_of(step, slot):
        base = step * TM_ROWS
        return _RowGather(
            ys_ref,
            lambda r: [(pos[TOP_K * (base + r) + k], ybuf.at[slot, k]) for k in range(TOP_K)],
            sem.at[slot], TM_ROWS)

    slot = _prefetched_gather(n, pl.num_programs(0), gather_of)

    xn = x_ref[...] + (meta_ref[:, 4:5] * ybuf[slot, 0] + meta_ref[:, 5:6] * ybuf[slot, 1])
    h = _rms(xn) * g_ref[...]
    if emit_x:
        out_refs[0][...] = xn
        out_refs[1][...] = h.astype(out_refs[1].dtype)
    else:
        out_refs[0][...] = h.astype(out_refs[0].dtype)


def _combine(pos_flat, x, meta, g, ys, emit_x):
    t, d = x.shape
    blk = pl.BlockSpec((TM_ROWS, d), lambda i, p: (i, 0))
    if emit_x:
        out_specs = [blk, blk]
        out_shape = [jax.ShapeDtypeStruct((t, d), F32), jax.ShapeDtypeStruct((t, d), BF16)]
    else:
        out_specs = [blk]
        out_shape = [jax.ShapeDtypeStruct((t, d), F32)]
    return pl.pallas_call(
        functools.partial(_combine_body, emit_x=emit_x),
        grid_spec=pltpu.PrefetchScalarGridSpec(
            num_scalar_prefetch=1,
            grid=(t // TM_ROWS,),
            in_specs=[blk, pl.BlockSpec((TM_ROWS, LANES), lambda i, p: (i, 0)),
                      pl.BlockSpec((1, d), lambda i, p: (0, 0)),
                      pl.BlockSpec(memory_space=pl.ANY)],
            out_specs=out_specs,
            scratch_shapes=[pltpu.VMEM((2, TOP_K, TM_ROWS, d), F32),
                            pltpu.SemaphoreType.DMA((2,))]),
        out_shape=out_shape,
        compiler_params=_params(("arbitrary",)),
        name="combine",
    )(pos_flat, x, meta, g.reshape(1, d), ys)


def _for_used_rows(n_sub, compute, o_ref):
    n_max = TM_MOE // TM_SUB
    for m in range(n_max + 1):
        @pl.when(n_sub == m)
        def _(m=m):
            if m > 0:
                compute(pl.ds(0, m * TM_SUB))
            if m < n_max and o_ref is not None:
                o_ref[pl.ds(m * TM_SUB, (n_max - m) * TM_SUB), :] = jnp.zeros(
                    ((n_max - m) * TM_SUB, o_ref.shape[1]), o_ref.dtype)


def _gmm_up_body(it_e, it_jw, it_xblk, it_oblk, it_jo, it_first, it_nsub,
                 x_ref, w1_ref, w3_ref, o_ref, wbf):
    del it_e, it_jw, it_xblk, it_oblk, it_jo
    n = pl.program_id(0)

    @pl.when(it_first[n] == 1)
    def _():
        wbf[0] = w1_ref[...].astype(BF16)
        wbf[1] = w3_ref[...].astype(BF16)

    _for_used_rows(it_nsub[n],
                   lambda rows: _up_compute(x_ref.at[rows, :], wbf, o_ref.at[rows, :], 2), o_ref)


def _gmm_up(items, xs, w1, w3, layer):
    n_rows, k = xs.shape
    n_ff = w1.shape[3]
    n_items = items[0].shape[0]

    def w_map(n, e, jw, xb, ob, jo, f, v):
        return (layer, e[n], 0, jw[n])

    return pl.pallas_call(
        _gmm_up_body,
        grid_spec=pltpu.PrefetchScalarGridSpec(
            num_scalar_prefetch=7,
            grid=(n_items,),
            in_specs=[pl.BlockSpec((TM_MOE, k), lambda n, e, jw, xb, ob, jo, f, v: (xb[n], 0)),
                      pl.BlockSpec((None, None, k, TN_UP2), w_map),
                      pl.BlockSpec((None, None, k, TN_UP2), w_map)],
            out_specs=pl.BlockSpec((TM_MOE, TN_UP2),
                                   lambda n, e, jw, xb, ob, jo, f, v: (ob[n], jo[n])),
            scratch_shapes=[pltpu.VMEM((2, k, TN_UP2), BF16)]),
        out_shape=jax.ShapeDtypeStruct((n_rows, n_ff), BF16),
        compiler_params=_params(("arbitrary",)),
        name="gmm_up",
    )(*items, xs, w1, w3)


def _gmm_down_body(t_e, t_ablk, t_nsub, a_ref, w_ref, o_ref):
    del t_e, t_ablk
    n = pl.program_id(0)
    k = pl.program_id(1)

    @pl.when(k == 0)
    def _():
        o_ref[...] = jnp.zeros_like(o_ref)

    def accumulate(rows):
        o_ref[rows, :] = jnp.dot(a_ref[rows, :], w_ref[...].astype(BF16),
                                 preferred_element_type=F32) + o_ref[rows, :]

    _for_used_rows(t_nsub[n], accumulate, None)


def _gmm_down(tiles, a, w2, layer):
    n_rows, n_ff = a.shape
    d = w2.shape[3]
    tk = TK_MOE
    nk = n_ff // tk
    n_tiles = tiles[0].shape[0]

    def kk(n, k, v):
        return jnp.where(v[n] > 0, k, nk - 1)

    return pl.pallas_call(
        _gmm_down_body,
        grid_spec=pltpu.PrefetchScalarGridSpec(
            num_scalar_prefetch=3,
            grid=(n_tiles, nk),
            in_specs=[pl.BlockSpec((TM_MOE, tk), lambda n, k, e, b, v: (b[n], kk(n, k, v))),
                      pl.BlockSpec((None, None, tk, d),
                                   lambda n, k, e, b, v: (layer, e[n], kk(n, k, v), 0))],
            out_specs=pl.BlockSpec((TM_MOE, d), lambda n, k, e, b, v: (n, 0))),
        out_shape=jax.ShapeDtypeStruct((n_rows, d), F32),
        compiler_params=_params(("arbitrary", "arbitrary")),
        name="gmm_down",
    )(*tiles, a, w2)


def _moe_plan(meta, counts, n_tokens, n_col_blocks):
    nt_max = (TOP_K * n_tokens) // TM_MOE + N_EXPERTS
    n_rows = nt_max * TM_MOE
    cnt = counts[0, :N_EXPERTS].astype(jnp.int32)
    tiles = (cnt + TM_MOE - 1) // TM_MOE
    tiles_end = jnp.cumsum(tiles)
    tiles_start = tiles_end - tiles
    n_tiles = tiles_end[-1]

    ids = meta[:, 0:2].astype(jnp.int32)
    rank = meta[:, 2:4].astype(jnp.int32)
    pos = (tiles_start[ids] * TM_MOE + rank).reshape(-1)
    token = jnp.repeat(jnp.arange(n_tokens, dtype=jnp.int32), TOP_K)
    tok_of_row = jnp.zeros((n_rows,), jnp.int32).at[pos].set(token, unique_indices=True)

    def owner(idx, ends):
        return jnp.sum((ends[None, :] <= idx[:, None]).astype(jnp.int32), axis=1)

    def rows_in_tile(e, local_tile, valid):
        return jnp.where(valid, jnp.minimum(cnt[e] - local_tile * TM_MOE, TM_MOE), 0)

    def sub_blocks(rows):
        return ((rows + TM_SUB - 1) // TM_SUB).astype(jnp.int32)

    tile_idx = jnp.arange(nt_max, dtype=jnp.int32)
    t_valid = tile_idx < n_tiles
    t_idx = jnp.minimum(tile_idx, n_tiles - 1)
    t_e = owner(t_idx, tiles_end)
    t_rows = rows_in_tile(t_e, t_idx - tiles_start[t_e], t_valid)
    tiles_plan = (t_e, t_idx, sub_blocks(t_rows))

    per_tile = TM_MOE // TM_DISPATCH
    blk_idx = jnp.arange(nt_max * per_tile, dtype=jnp.int32)
    blk_active = (blk_idx % per_tile) * TM_DISPATCH < t_rows[blk_idx // per_tile]
    n_active = jnp.sum(blk_active.astype(jnp.int32))
    place = jnp.where(blk_active, jnp.cumsum(blk_active) - 1,
                      n_active + jnp.cumsum(jnp.logical_not(blk_active)) - 1)
    blk_order = jnp.sum(jnp.where(place[None, :] == blk_idx[:, None], blk_idx[None, :], 0),
                        axis=1).astype(jnp.int32)
    dispatch_plan = (tok_of_row, blk_order, n_active.reshape(1).astype(jnp.int32))

    item_idx = jnp.arange(nt_max * n_col_blocks, dtype=jnp.int32)
    n_items = n_tiles * n_col_blocks
    i_valid = item_idx < n_items
    i_idx = jnp.minimum(item_idx, n_items - 1)
    i_e = owner(i_idx, tiles_end * n_col_blocks)
    local = i_idx - tiles_start[i_e] * n_col_blocks
    te = jnp.maximum(tiles[i_e], 1)
    i_j = (local // te).astype(jnp.int32)
    i_l = local % te
    i_blk = (tiles_start[i_e] + i_l).astype(jnp.int32)
    i_first = jnp.logical_and(i_l == 0, i_valid)
    tail = jnp.maximum(item_idx - n_items, 0)
    o_blk = jnp.where(i_valid, i_blk, n_tiles + tail // n_col_blocks).astype(jnp.int32)
    o_j = jnp.where(i_valid, i_j, tail % n_col_blocks).astype(jnp.int32)
    items_plan = (i_e, i_j, i_blk, o_blk, o_j, i_first.astype(jnp.int32),
                  sub_blocks(rows_in_tile(i_e, i_l, i_valid)))
    return pos, dispatch_plan, tiles_plan, items_plan


def _moe(x, g, router_w, w1, w3, w2, layer, g_next, emit_x):
    t = x.shape[0]
    meta, counts = _router(x, g, router_w)
    pos, dispatch_plan, tiles_plan, items_plan = _moe_plan(meta, counts, t, w1.shape[3] // TN_UP2)
    xs = _dispatch(*dispatch_plan, x, g)
    a = _gmm_up(items_plan, xs, w1, w3, layer)
    ys = _gmm_down(tiles_plan, a, w2, layer)
    return _combine(pos, x, meta, g_next, ys, emit_x)


def _mix_params(spatial_w, spatial_b, sample_len):
    mask = jnp.tril(jnp.ones((CHUNK, CHUNK), dtype=bool))
    w = jnp.where(mask[None], spatial_w, 0)
    b_prompt = jnp.broadcast_to(spatial_b[:, :, None], (H_C, CHUNK, HEAD_DIM))
    eye = jnp.eye(CHUNK // sample_len, dtype=w.dtype)
    w_small = w[:, :sample_len, :sample_len]
    w_sample = jnp.einsum("ab,hts->hatbs", eye, w_small).reshape(H_C, CHUNK, CHUNK)
    b_small = jnp.tile(spatial_b[:, :sample_len], (1, CHUNK // sample_len))
    b_sample = jnp.broadcast_to(b_small[:, :, None], (H_C, CHUNK, HEAD_DIM))
    return w.astype(BF16), b_prompt, w_sample.astype(BF16), b_sample


def kernel(x_prompt, x_sample, state_conv_a, state_conv_b, norm1_g, norm2_g, final_g, w_in, w_out,
           conv_a_w, conv_a_b, ln_a_g, ln_a_b, conv_b_w, ln_c_g, ln_c_b, spatial_w, spatial_b,
           group_g, ffn_w1, ffn_w3, ffn_w2, router_w, moe_w1, moe_w3, moe_w2):
    n_batch, seq_len, d = x_prompt.shape
    n_dec, dec_len, _ = x_sample.shape
    n_prompt = n_batch * seq_len
    assert dec_len == SUBLANES and d == D_MODEL

    x = jnp.concatenate([x_prompt.reshape(n_prompt, d), x_sample.reshape(n_dec * dec_len, d)], axis=0)
    h = _norm(x, norm1_g[0])

    glus, pres, vs = [], [], []
    for l in range(DEPTH):
        mw_p, mb_p, mw_s, mb_s = _mix_params(spatial_w[l], spatial_b[l], dec_len)
        lw = dict(caw=conv_a_w[l], cab=conv_a_b[l].reshape(1, D_A),
                  lag=ln_a_g[l].reshape(1, D_A), lab=ln_a_b[l].reshape(1, D_A),
                  cbw=conv_b_w[l], lcg=ln_c_g[l].reshape(1, D_C),
                  lcb=ln_c_b[l].reshape(1, D_C), gg=group_g[l].reshape(1, D_MODEL),
                  mixw_p=mw_p, mixb_p=mb_p, mixw_s=mw_s, mixb_s=mb_s)
        z = _mm_up(h, [w_in], l, F32)
        y, glu, pre, v = _mixers(z, state_conv_a, state_conv_b, l, lw, n_prompt, seq_len)
        glus.append(glu)
        pres.append(pre)
        vs.append(v)
        last = l == DEPTH - 1
        g_next = final_g if last else norm1_g[l + 1]
        i = l // 2
        if l % 2 == 0:
            x, h = _mm_down(y, w_out, l, x, norm2_g[l])
            a = _mm_up(h, [ffn_w1, ffn_w3], i, BF16)
            x, h = _mm_down(a, ffn_w2, i, x, g_next)
        else:
            x = _mm_down(y, w_out, l, x)
            outs = _moe(x, norm2_g[l], router_w[i], moe_w1, moe_w3, moe_w2, i, g_next, not last)
            if last:
                y_final = outs[0]
            else:
                x, h = outs

    def prompt_rows(a):
        return a[:n_prompt].reshape(n_batch, seq_len, a.shape[-1])

    def sample_rows(a):
        return a[n_prompt:].reshape(n_dec, dec_len, a.shape[-1])

    ka, kb = CONV_A_WIDTH - 1, CONV_B_WIDTH - 1
    chunk_start = ((seq_len - 1) // CHUNK) * CHUNK
    pa = jnp.stack([prompt_rows(g)[:, seq_len - ka:] for g in glus])
    pb = jnp.stack([prompt_rows(p)[:, seq_len - kb:] for p in pres])
    pv = jnp.stack([prompt_rows(v)[:, chunk_start:] for v in vs])
    sa = jnp.stack([jnp.concatenate([state_conv_a[l][:, dec_len:], sample_rows(glus[l])], axis=1)
                    for l in range(DEPTH)])
    sb = jnp.stack([sample_rows(p)[:, dec_len - kb:] for p in pres])
    sv = jnp.stack([sample_rows(v) for v in vs])
    return (prompt_rows(y_final), sample_rows(y_final), pa, pb, pv, sa, sb, sv)
```
